```python
import jax, jax.numpy as jnp
from jax import lax
import numpy as np

D_MODEL = 4096
BATCH = 2
SEQ = 8192
DEPTH = 2

BRANCH_WIDTH = D_MODEL // 2
N_BRANCHES = 3
CONV_WIDTH = 3
NORM_EPS = 1e-6
RWKV_HEAD_DIM = 64
RWKV_HEADS = BRANCH_WIDTH // RWKV_HEAD_DIM
RWKV_LORA_W = max(32, int(round(1.8 * BRANCH_WIDTH ** 0.5 / 32)) * 32)
RWKV_LORA_A = max(32, int(round(1.8 * BRANCH_WIDTH ** 0.5 / 32)) * 32)
RWKV_LORA_G = max(32, int(round(0.6 * BRANCH_WIDTH ** 0.8 / 32)) * 32)
RWKV_GN_EPS = 1e-5 * RWKV_HEAD_DIM
GLA_HEADS = 4
GLA_KEY_WIDTH = BRANCH_WIDTH // 2
GLA_DK = GLA_KEY_WIDTH // GLA_HEADS
GLA_DV = BRANCH_WIDTH // GLA_HEADS
GLA_GATE_RANK = 16
GLA_GATE_NORM = 16.0
GLA_CHUNK = 64
GLA_NORM_EPS = 1e-5
D_FF = 11008
RWKV_SPLITS = (BRANCH_WIDTH, BRANCH_WIDTH, BRANCH_WIDTH, RWKV_LORA_W, RWKV_LORA_A, RWKV_LORA_G)
GLA_SPLITS = (GLA_KEY_WIDTH, GLA_KEY_WIDTH, BRANCH_WIDTH, GLA_GATE_RANK, BRANCH_WIDTH)
CONV_SPLITS = (BRANCH_WIDTH, BRANCH_WIDTH, BRANCH_WIDTH)
GATE_COLS = N_BRANCHES * D_MODEL
IN_SPLITS = (sum(RWKV_SPLITS), sum(GLA_SPLITS), sum(CONV_SPLITS), GATE_COLS)
IN_COLS = sum(IN_SPLITS)

kernel_name = 'hybrid_rwkv7_gla_shortconv_block'


def _split(t, sizes):
    idx = np.cumsum(sizes)[:-1].tolist()
    return jnp.split(t, idx, axis=-1)


def rms_norm(x, g):
    xf = x.astype(jnp.float32)
    y = xf * lax.rsqrt(jnp.mean(xf * xf, axis=-1, keepdims=True) + NORM_EPS)
    return (y * g.astype(jnp.float32)).astype(x.dtype)


def shift_prev(u):
    return jnp.pad(u, ((0, 0), (1, 0), (0, 0)))[:, :-1]


def causal_conv3(u, w):
    s = u.shape[1]
    up = jnp.pad(u, ((0, 0), (CONV_WIDTH - 1, 0), (0, 0)))
    return up[:, :s] * w[0] + up[:, 1:s + 1] * w[1] + up[:, 2:] * w[2]


def rwkv7_time_mix(p, mu, w0, w_w2, a0, w_a2, w_g2, k_k, k_a, r_k, ln_w, ln_b):
    bsz, s, _ = p.shape
    dt = p.dtype
    f32 = jnp.float32
    p = p + (shift_prev(p) - p) * mu
    r, k, v, wd, ad, gd = _split(p, RWKV_SPLITS)
    w_pre = (w0 + jnp.tanh(wd) @ w_w2).astype(f32)
    decay = jnp.exp(-jnp.exp(-jax.nn.softplus(-w_pre) - 0.5))
    a = jax.nn.sigmoid((a0 + ad @ w_a2).astype(f32))
    g = (jax.nn.sigmoid(gd) @ w_g2).astype(f32)
    heads = lambda t: t.astype(f32).reshape(bsz, s, RWKV_HEADS, RWKV_HEAD_DIM)
    r, k, v, decay, a = heads(r), heads(k), heads(v), heads(decay), heads(a)
    kk = k * k_k.astype(f32).reshape(RWKV_HEADS, RWKV_HEAD_DIM)
    kk = kk / jnp.maximum(jnp.sqrt(jnp.sum(kk * kk, axis=-1, keepdims=True)), 1e-12)
    k = k * (1.0 + (a - 1.0) * k_a.astype(f32).reshape(RWKV_HEADS, RWKV_HEAD_DIM))

    def step(state, inp):
        r_t, w_t, k_t, v_t, kk_t, a_t = inp
        sa = jnp.einsum('bhvk,bhk->bhv', state, -kk_t)
        state = (state * w_t[:, :, None, :] + sa[..., None] * (kk_t * a_t)[:, :, None, :]
                 + v_t[..., None] * k_t[:, :, None, :])
        return state, jnp.einsum('bhvk,bhk->bhv', state, r_t)

    xs = tuple(jnp.swapaxes(t, 0, 1) for t in (r, decay, k, v, kk, a))
    s0 = jnp.zeros((bsz, RWKV_HEADS, RWKV_HEAD_DIM, RWKV_HEAD_DIM), f32)
    _, y = lax.scan(step, s0, xs)
    y = jnp.swapaxes(y, 0, 1)
    mean = jnp.mean(y, axis=-1, keepdims=True)
    var = jnp.mean(jnp.square(y - mean), axis=-1, keepdims=True)
    y = ((y - mean) * lax.rsqrt(var + RWKV_GN_EPS)).reshape(bsz, s, BRANCH_WIDTH)
    y = y * ln_w.astype(f32) + ln_b.astype(f32)
    bonus = jnp.sum(r * k * r_k.astype(f32), axis=-1, keepdims=True) * v
    y = (y + bonus.reshape(bsz, s, BRANCH_WIDTH)) * g
    return y.astype(dt)


def gla_mix(p, w_a2, b_a, norm_g):
    bsz, s, _ = p.shape
    dt = p.dtype
    f32 = jnp.float32
    q, k, v, ad, go = _split(p, GLA_SPLITS)
    gk = jax.nn.log_sigmoid((ad @ w_a2 + b_a).astype(f32)) / GLA_GATE_NORM
    nc = s // GLA_CHUNK

    def chunks(t, d):
        return t.astype(f32).reshape(bsz, nc, GLA_CHUNK, GLA_HEADS, d).transpose(0, 3, 1, 2, 4)

    q = chunks(q, GLA_DK) * (GLA_DK ** -0.5)
    k = chunks(k, GLA_DK)
    v = chunks(v, GLA_DV)
    b = jnp.cumsum(chunks(gk, GLA_DK), axis=3)
    q_dec = q * jnp.exp(b)
    k_inv = k * jnp.exp(-b)
    causal = jnp.tril(jnp.ones((GLA_CHUNK, GLA_CHUNK), dtype=bool))
    att = jnp.where(causal, jnp.einsum('bhncd,bhnjd->bhncj', q_dec, k_inv), 0.0)
    o_intra = jnp.einsum('bhncj,bhnjv->bhncv', att, v)
    b_last = b[:, :, :, -1:, :]
    k_dec = k * jnp.exp(b_last - b)
    chunk_decay = jnp.exp(b_last[:, :, :, 0, :])

    def step(state, inp):
        q_n, k_n, v_n, dec_n = inp
        o_n = jnp.einsum('bhcd,bhdv->bhcv', q_n, state)
        state = state * dec_n[..., None] + jnp.einsum('bhcd,bhcv->bhdv', k_n, v_n)
        return state, o_n

    xs = tuple(jnp.moveaxis(t, 2, 0) for t in (q_dec, k_dec, v, chunk_decay))
    s0 = jnp.zeros((bsz, GLA_HEADS, GLA_DK, GLA_DV), f32)
    _, o_inter = lax.scan(step, s0, xs)
    o = o_intra + jnp.moveaxis(o_inter, 0, 2)
    o = o.transpose(0, 2, 3, 1, 4).reshape(bsz, s, GLA_HEADS, GLA_DV)
    o = o * lax.rsqrt(jnp.mean(o * o, axis=-1, keepdims=True) + GLA_NORM_EPS) * norm_g.astype(f32)
    o = o.reshape(bsz, s, BRANCH_WIDTH) * jax.nn.silu(go.astype(f32))
    return o.astype(dt)


def short_conv_mix(p, conv_w):
    bg, cg, h = _split(p, CONV_SPLITS)
    return bg * causal_conv3(cg * h, conv_w)


def hybrid_layer(x, pre_mix_g, w_in, rwkv_mu, rwkv_w0, rwkv_w_w2, rwkv_a0, rwkv_w_a2, rwkv_w_g2,
                 rwkv_k_k, rwkv_k_a, rwkv_r_k, rwkv_ln_w, rwkv_ln_b, gla_w_a2, gla_b_a, gla_norm_g,
                 sconv_w, w_br, w_o, post_mix_g, pre_ffn_g, w_up, ffn_conv_w, w_down, post_ffn_g):
    bsz, s, _ = x.shape
    h = rms_norm(x, pre_mix_g)
    p = h @ w_in
    p_rwkv, p_gla, p_conv, p_gate = _split(p, IN_SPLITS)
    y_a = rwkv7_time_mix(p_rwkv, rwkv_mu, rwkv_w0, rwkv_w_w2, rwkv_a0, rwkv_w_a2, rwkv_w_g2,
                         rwkv_k_k, rwkv_k_a, rwkv_r_k, rwkv_ln_w, rwkv_ln_b)
    y_b = gla_mix(p_gla, gla_w_a2, gla_b_a, gla_norm_g)
    y_c = short_conv_mix(p_conv, sconv_w)
    gates = jax.nn.sigmoid(p_gate.astype(jnp.float32)).astype(x.dtype).reshape(bsz, s, N_BRANCHES, D_MODEL)
    merged = (gates[:, :, 0] * (y_a @ w_br[0]) + gates[:, :, 1] * (y_b @ w_br[1])
              + gates[:, :, 2] * (y_c @ w_br[2]))
    x = x + rms_norm(merged @ w_o, post_mix_g)
    h = rms_norm(x, pre_ffn_g)
    u = causal_conv3(h @ w_up, ffn_conv_w)
    u_gate, u_val = _split(u, (D_FF, D_FF))
    act = jax.nn.gelu(u_gate, approximate=True) * u_val
    x = x + rms_norm(act @ w_down, post_ffn_g)
    return x


def setup_inputs(seed: int = 0) -> dict:
    key = jax.random.key(seed)
    ks = jax.random.split(key, 26)
    f32 = jnp.float32
    L = DEPTH
    E = BRANCH_WIDTH

    def nrm(k, shape, scale):
        return jax.random.normal(k, shape, f32) * scale

    def gain(k, shape):
        return 1.0 + 0.05 * jax.random.normal(k, shape, f32)

    return {
        'x': nrm(ks[0], (BATCH, SEQ, D_MODEL), 1.0),
        'pre_mix_g': gain(ks[1], (L, D_MODEL)),
        'w_in': nrm(ks[2], (L, D_MODEL, IN_COLS), D_MODEL ** -0.5),
        'rwkv_mu': jax.random.uniform(ks[3], (L, IN_SPLITS[0]), f32),
        'rwkv_w0': jax.random.uniform(ks[4], (L, E), f32, -6.0, 1.0),
        'rwkv_w_w2': nrm(ks[5], (L, RWKV_LORA_W, E), 0.5 * RWKV_LORA_W ** -0.5),
        'rwkv_a0': nrm(ks[6], (L, E), 0.1),
        'rwkv_w_a2': nrm(ks[7], (L, RWKV_LORA_A, E), 0.5 * RWKV_LORA_A ** -0.5),
        'rwkv_w_g2': nrm(ks[8], (L, RWKV_LORA_G, E), RWKV_LORA_G ** -0.5),
        'rwkv_k_k': 0.85 + 0.05 * jax.random.normal(ks[9], (L, E), f32),
        'rwkv_k_a': gain(ks[10], (L, E)),
        'rwkv_r_k': nrm(ks[11], (L, RWKV_HEADS, RWKV_HEAD_DIM), 0.1),
        'rwkv_ln_w': gain(ks[12], (L, E)),
        'rwkv_ln_b': nrm(ks[13], (L, E), 0.01),
        'gla_w_a2': nrm(ks[14], (L, GLA_GATE_RANK, GLA_KEY_WIDTH), GLA_GATE_RANK ** -0.5),
        'gla_b_a': nrm(ks[15], (L, GLA_KEY_WIDTH), 0.01),
        'gla_norm_g': gain(ks[16], (L, GLA_DV)),
        'sconv_w': nrm(ks[17], (L, CONV_WIDTH, E), CONV_WIDTH ** -0.5),
        'w_br': nrm(ks[18], (L, N_BRANCHES, E, D_MODEL), E ** -0.5),
        'w_o': nrm(ks[19], (L, D_MODEL, D_MODEL), D_MODEL ** -0.5),
        'post_mix_g': gain(ks[20], (L, D_MODEL)),
        'pre_ffn_g': gain(ks[21], (L, D_MODEL)),
        'w_up': nrm(ks[22], (L, D_MODEL, 2 * D_FF), D_MODEL ** -0.5),
        'ffn_conv_w': nrm(ks[23], (L, CONV_WIDTH, 2 * D_FF), CONV_WIDTH ** -0.5),
        'w_down': nrm(ks[24], (L, D_FF, D_MODEL), D_FF ** -0.5),
        'post_ffn_g': gain(ks[25], (L, D_MODEL)),
    }


def reference(x, pre_mix_g, w_in, rwkv_mu, rwkv_w0, rwkv_w_w2, rwkv_a0, rwkv_w_a2, rwkv_w_g2,
              rwkv_k_k, rwkv_k_a, rwkv_r_k, rwkv_ln_w, rwkv_ln_b, gla_w_a2, gla_b_a, gla_norm_g,
              sconv_w, w_br, w_o, post_mix_g, pre_ffn_g, w_up, ffn_conv_w, w_down, post_ffn_g):
    for l in range(DEPTH):
        x = hybrid_layer(x, pre_mix_g[l], w_in[l], rwkv_mu[l], rwkv_w0[l], rwkv_w_w2[l], rwkv_a0[l],
                         rwkv_w_a2[l], rwkv_w_g2[l], rwkv_k_k[l], rwkv_k_a[l], rwkv_r_k[l],
                         rwkv_ln_w[l], rwkv_ln_b[l], gla_w_a2[l], gla_b_a[l], gla_norm_g[l],
                         sconv_w[l], w_br[l], w_o[l], post_mix_g[l], pre_ffn_g[l], w_up[l],
                         ffn_conv_w[l], w_down[l], post_ffn_g[l])
    return x
```

```python
import functools

import jax
import jax.numpy as jnp
from jax import lax
from jax.experimental import pallas as pl
from jax.experimental.pallas import tpu as pltpu

F32 = jnp.float32
BF16 = jnp.bfloat16

NORM_EPS = 1e-6
RWKV_HEAD_DIM = 64
RWKV_GN_EPS = 1e-5 * RWKV_HEAD_DIM
GLA_GATE_NORM = 16.0
GLA_NORM_EPS = 1e-5
CONV_WIDTH = 3
CHUNK = 64

V7X_LANES = 128
V7X_SUBLANES = 8
V7X_VMEM_BYTES = 64 * 1024 * 1024
VMEM_CAP_BYTES = V7X_VMEM_BYTES - 6 * 1024 * 1024


def _round_up(n, m):
    return (n + m - 1) // m * m


def _pick_tile(n, pref, quantum=V7X_LANES):
    if n <= pref:
        return n
    t = pref - pref % quantum
    while t >= quantum:
        if n % t == 0:
            return t
        t -= quantum
    return n


def _params(n_axes, block_bytes):
    limit = min(VMEM_CAP_BYTES, 2 * block_bytes + 16 * 1024 * 1024)
    return pltpu.CompilerParams(
        dimension_semantics=("arbitrary",) * n_axes, vmem_limit_bytes=int(limit))


def _nbytes(shape, dtype):
    n = 1
    for s in shape:
        n *= s
    return n * jnp.dtype(dtype).itemsize


def _dot(a, b):
    return jnp.dot(a, b, preferred_element_type=F32)


def _dot_nt(a, b):
    return lax.dot_general(a, b, (((1,), (1,)), ((), ())), preferred_element_type=F32)


def _dot_tn(a, b):
    return lax.dot_general(a, b, (((0,), (0,)), ((), ())), preferred_element_type=F32)


def _bf(x):
    return x.astype(BF16)


def _sigmoid(x):
    return 1.0 / (1.0 + jnp.exp(-x))


def _split3(x):
    hi = _bf(x)
    r1 = x - hi.astype(F32)
    mid = _bf(r1)
    lo = _bf(r1 - mid.astype(F32))
    return hi, mid, lo


def _chunk_cumsum(x, tril_bf):
    hi, mid, lo = _split3(x)
    return _dot(tril_bf, hi) + _dot(tril_bf, mid) + _dot(tril_bf, lo)


def _shift_rows(u, prev8, k):
    rolled = pltpu.roll(u, k, 0)
    rolled_prev = pltpu.roll(prev8, k, 0)
    row = lax.broadcasted_iota(jnp.int32, prev8.shape, 0)
    top = jnp.where(row < k, rolled_prev, rolled[:V7X_SUBLANES])
    return jnp.concatenate([top, rolled[V7X_SUBLANES:]], axis=0)


def _rms_kernel(x_ref, g_ref, h_ref):
    x = x_ref[...]
    ms = jnp.mean(x * x, axis=-1, keepdims=True)
    h_ref[...] = (x * lax.rsqrt(ms + NORM_EPS) * g_ref[...]).astype(h_ref.dtype)


def _rms_norm_bf16(x, g):
    t, d = x.shape
    tm = _pick_tile(t, 256, V7X_SUBLANES)
    blk = _nbytes((tm, d), F32) + _nbytes((tm, d), BF16)
    return pl.pallas_call(
        _rms_kernel,
        out_shape=jax.ShapeDtypeStruct((t, d), BF16),
        grid=(t // tm,),
        in_specs=[pl.BlockSpec((tm, d), lambda i: (i, 0)),
                  pl.BlockSpec((1, d), lambda i: (0, 0))],
        out_specs=pl.BlockSpec((tm, d), lambda i: (i, 0)),
        compiler_params=_params(1, blk),
        name="rms_norm",
    )(x, g.reshape(1, d))


def _residual_kernel(o_ref, x_ref, gpost_ref, *rest, with_next):
    o = o_ref[...]
    ms = jnp.mean(o * o, axis=-1, keepdims=True)
    xn = x_ref[...] + o * lax.rsqrt(ms + NORM_EPS) * gpost_ref[...]
    if with_next:
        gpre_ref, xo_ref, h_ref = rest
        ms2 = jnp.mean(xn * xn, axis=-1, keepdims=True)
        h_ref[...] = (xn * lax.rsqrt(ms2 + NORM_EPS) * gpre_ref[...]).astype(h_ref.dtype)
    else:
        (xo_ref,) = rest
    xo_ref[...] = xn


def _residual_norm(o, x, g_post, g_next):
    t, d = x.shape
    tm = _pick_tile(t, 256, V7X_SUBLANES)
    with_next = g_next is not None
    row = pl.BlockSpec((tm, d), lambda i: (i, 0))
    vec = pl.BlockSpec((1, d), lambda i: (0, 0))
    in_specs = [row, row, vec] + ([vec] if with_next else [])
    args = [o, x, g_post.reshape(1, d)] + ([g_next.reshape(1, d)] if with_next else [])
    out_shape = [jax.ShapeDtypeStruct((t, d), F32)]
    out_specs = [row]
    if with_next:
        out_shape.append(jax.ShapeDtypeStruct((t, d), BF16))
        out_specs.append(row)
    blk = 3 * _nbytes((tm, d), F32) + _nbytes((tm, d), BF16)
    res = pl.pallas_call(
        functools.partial(_residual_kernel, with_next=with_next),
        out_shape=out_shape,
        grid=(t // tm,),
        in_specs=in_specs,
        out_specs=out_specs,
        compiler_params=_params(1, blk),
        name="residual_norm",
    )(*args)
    return (res[0], res[1]) if with_next else (res[0], None)


def _mm_kernel(x_ref, w_ref, o_ref, *, act):
    acc = _dot(x_ref[...], w_ref[...])
    if act == "sigmoid":
        acc = _sigmoid(acc)
    o_ref[...] = acc.astype(o_ref.dtype)


def _matmul(x, w, *, out_dtype, act=None, tm=1024, tn=1024, weight_stationary=True, name="matmul"):
    m, k = x.shape
    _, n = w.shape
    tm = _pick_tile(m, tm, V7X_SUBLANES)
    tn = _pick_tile(n, tn)
    if weight_stationary:
        grid = (n // tn, m // tm)
        xm, wm, om = (lambda j, i: (i, 0)), (lambda j, i: (0, j)), (lambda j, i: (i, j))
    else:
        grid = (m // tm, n // tn)
        xm, wm, om = (lambda i, j: (i, 0)), (lambda i, j: (0, j)), (lambda i, j: (i, j))
    blk = _nbytes((tm, k), x.dtype) + _nbytes((k, tn), w.dtype) + _nbytes((tm, tn), out_dtype) \
        + _nbytes((tm, tn), F32)
    return pl.pallas_call(
        functools.partial(_mm_kernel, act=act),
        out_shape=jax.ShapeDtypeStruct((m, n), out_dtype),
        grid=grid,
        in_specs=[pl.BlockSpec((tm, k), xm), pl.BlockSpec((k, tn), wm)],
        out_specs=pl.BlockSpec((tm, tn), om),
        compiler_params=_params(2, blk),
        name=name,
    )(x, w)


def _rwkv_kernel(p_ref, mu_ref, w0_ref, ww2_ref, a0_ref, wa2_ref, wg2_ref, kk_ref, ka_ref, rk_ref,
                 lnw_ref, lnb_ref, y_ref, prev_ref, state_ref, *, e, lwp, lap):
    c = p_ref.shape[0]
    pair = 2 * RWKV_HEAD_DIM
    n_pairs = e // pair

    @pl.when(pl.program_id(1) == 0)
    def _():
        prev_ref[...] = jnp.zeros_like(prev_ref)
        state_ref[...] = jnp.zeros_like(state_ref)

    p = p_ref[...]
    shifted = _shift_rows(p, prev_ref[...], 1)
    prev_ref[...] = p[c - V7X_SUBLANES:]
    xm = p + (shifted - p) * mu_ref[...]

    r = xm[:, 0:e]
    k = xm[:, e:2 * e]
    v = xm[:, 2 * e:3 * e]
    o = 3 * e
    wd = xm[:, o:o + lwp]
    ad = xm[:, o + lwp:o + lwp + lap]
    gd = xm[:, o + lwp + lap:]

    w_pre = w0_ref[...] + _dot(_bf(jnp.tanh(wd)), ww2_ref[...])
    lw = (-0.6065306597126334) * _sigmoid(w_pre)
    a = _sigmoid(a0_ref[...] + _dot(_bf(ad), wa2_ref[...]))
    g = _dot(_bf(_sigmoid(gd)), wg2_ref[...])

    ri = lax.broadcasted_iota(jnp.int32, (pair, pair), 0)
    ci = lax.broadcasted_iota(jnp.int32, (pair, pair), 1)
    same_head = (ri < RWKV_HEAD_DIM) == (ci < RWKV_HEAD_DIM)
    head_ones = jnp.where(same_head, 1.0, 0.0).astype(BF16)

    def head_sum(x):
        hi = _bf(x)
        lo = _bf(x - hi.astype(F32))
        xs = jnp.concatenate([hi, lo], axis=0)
        s = jnp.concatenate(
            [_dot(xs[:, j * pair:(j + 1) * pair], head_ones) for j in range(n_pairs)], axis=1)
        return s[:c] + s[c:]

    kk = k * kk_ref[...]
    kk = kk / jnp.maximum(jnp.sqrt(head_sum(kk * kk)), 1e-12)
    kmod = k * (1.0 + (a - 1.0) * ka_ref[...])
    bvec = kk * a

    tri_r = lax.broadcasted_iota(jnp.int32, (c, c), 0)
    tri_c = lax.broadcasted_iota(jnp.int32, (c, c), 1)
    tril = jnp.where(tri_c <= tri_r, 1.0, 0.0).astype(BF16)
    gcum = _chunk_cumsum(lw, tril)
    glast = gcum[c - 1:c, :]
    e_pos = jnp.exp(gcum)
    e_neg = jnp.exp(-gcum)
    e_tail = jnp.exp(glast - gcum)
    r_t = r * e_pos
    kap_t = kk * jnp.exp(gcum - lw)
    b_t = bvec * e_neg
    k_t = kmod * e_neg
    b_dec = bvec * e_tail
    k_dec = kmod * e_tail
    gamma_last = jnp.exp(glast)

    c2 = 2 * c
    mi = lax.broadcasted_iota(jnp.int32, (c2, c2), 0)
    mj = lax.broadcasted_iota(jnp.int32, (c2, c2), 1)
    same_blk = (mi < c) == (mj < c)
    ti = jnp.where(mi < c, mi, mi - c)
    tj = jnp.where(mj < c, mj, mj - c)
    m_strict = same_blk & (tj < ti)
    m_incl = same_blk & (tj <= ti)
    m_16 = same_blk & ((ti >> 4) == (tj >> 4))
    m_32 = same_blk & ((ti >> 5) == (tj >> 5))
    eye = jnp.where(mi == mj, 1.0, 0.0).astype(F32)
    lane_lo = lax.broadcasted_iota(jnp.int32, (c, pair), 1) < RWKV_HEAD_DIM

    def stack2(x):
        return jnp.concatenate([jnp.where(lane_lo, x, 0.0), jnp.where(lane_lo, 0.0, x)], axis=0)

    y_cols = []
    for j in range(n_pairs):
        sl = slice(j * pair, (j + 1) * pair)
        kap_s = _bf(stack2(kap_t[:, sl]))
        r_s = _bf(stack2(r_t[:, sl]))
        v_s = _bf(stack2(v[:, sl]))
        lhs = jnp.concatenate([kap_s, r_s], axis=0)
        btj = _bf(b_t[:, sl])
        ktj = _bf(k_t[:, sl])
        rhs = jnp.concatenate([btj, btj, ktj, ktj], axis=0)
        scores = _dot_nt(lhs, rhs)
        a_m = jnp.where(m_strict, scores[:c2, :c2], 0.0)
        b_m = jnp.where(m_strict, scores[:c2, c2:], 0.0)
        rb_m = jnp.where(m_incl, scores[c2:, :c2], 0.0)
        rk_m = jnp.where(m_incl, scores[c2:, c2:], 0.0)

        st = state_ref[j]
        proj = _dot_nt(lhs, _bf(st))
        rhs_u = -(proj[:c2] + _dot(_bf(b_m), v_s))

        d16 = jnp.where(m_16, a_m, 0.0)
        e32 = jnp.where(m_32, a_m, 0.0) - d16
        e64 = a_m - d16 - e32
        d16b = _bf(d16)
        t_inv = eye - d16
        dp = _dot(d16b, d16b)
        t_inv = t_inv + _dot(_bf(t_inv), _bf(dp))
        dp = _dot(_bf(dp), _bf(dp))
        t_inv = t_inv + _dot(_bf(t_inv), _bf(dp))
        dp = _dot(_bf(dp), _bf(dp))
        t_inv = t_inv + _dot(_bf(t_inv), _bf(dp))
        t_b = _bf(t_inv)
        t32 = t_inv - _dot(_bf(_dot(t_b, _bf(e32))), t_b)
        t32b = _bf(t32)
        z = _dot(t32b, _bf(rhs_u))
        u = z - _dot(t32b, _bf(_dot(_bf(e64), _bf(z))))

        uv = jnp.concatenate([_bf(u), v_s], axis=0)
        y_bd = proj[c2:] + _dot(_bf(jnp.concatenate([rb_m, rk_m], axis=1)), uv)
        y_cols.append(y_bd[:c] + y_bd[c:])

        bk = jnp.concatenate([_bf(stack2(b_dec[:, sl])), _bf(stack2(k_dec[:, sl]))], axis=0)
        state_ref[j] = st * gamma_last[:, sl] + _dot_tn(uv, bk)

    y = jnp.concatenate(y_cols, axis=1)
    inv_n = 1.0 / RWKV_HEAD_DIM
    mean = head_sum(y) * inv_n
    yc = y - mean
    var = head_sum(yc * yc) * inv_n
    yn = yc * lax.rsqrt(var + RWKV_GN_EPS) * lnw_ref[...] + lnb_ref[...]
    bonus = head_sum(r * kmod * rk_ref[...]) * v
    y_ref[...] = ((yn + bonus) * g).astype(y_ref.dtype)


def _rwkv_mix(p, mu, w0, ww2, a0, wa2, wg2, k_k, k_a, r_k, ln_w, ln_b, *, batch, seq, e):
    t, width = p.shape
    lwp, lap = ww2.shape[0], wa2.shape[0]
    nc = seq // CHUNK
    n_pairs = e // (2 * RWKV_HEAD_DIM)
    vec = lambda n: pl.BlockSpec((1, n), lambda b, i: (0, 0))
    full = lambda a: pl.BlockSpec(a.shape, lambda b, i: (0, 0))
    blk = _nbytes((CHUNK, width), F32) + _nbytes((CHUNK, e), BF16) \
        + sum(_nbytes(a.shape, a.dtype) for a in (ww2, wa2, wg2)) + 24 * _nbytes((CHUNK, e), F32)
    return pl.pallas_call(
        functools.partial(_rwkv_kernel, e=e, lwp=lwp, lap=lap),
        out_shape=jax.ShapeDtypeStruct((t, e), BF16),
        grid=(batch, nc),
        in_specs=[pl.BlockSpec((CHUNK, width), lambda b, i: (b * nc + i, 0)),
                  vec(width), vec(e), full(ww2), vec(e), full(wa2), full(wg2),
                  vec(e), vec(e), vec(e), vec(e), vec(e)],
        out_specs=pl.BlockSpec((CHUNK, e), lambda b, i: (b * nc + i, 0)),
        scratch_shapes=[pltpu.VMEM((V7X_SUBLANES, width), F32),
                        pltpu.VMEM((n_pairs, 2 * RWKV_HEAD_DIM, 2 * RWKV_HEAD_DIM), F32)],
        compiler_params=_params(2, blk),
        name="rwkv7_mix",
    )(p, mu, w0, ww2, a0, wa2, wg2, k_k, k_a, r_k, ln_w, ln_b)


def _gla_kernel(p_ref, wa2_ref, ba_ref, ng_ref, y_ref, state_ref, *, heads, dk, dv, adp):
    c = p_ref.shape[0]
    kw = heads * dk
    e = heads * dv

    @pl.when(pl.program_id(1) == 0)
    def _():
        state_ref[...] = jnp.zeros_like(state_ref)

    p = p_ref[...]
    q = p[:, 0:kw]
    k = p[:, kw:2 * kw]
    v = p[:, 2 * kw:2 * kw + e]
    ad = p[:, 2 * kw + e:2 * kw + e + adp]
    go = p[:, 2 * kw + e + adp:]

    z = _dot(_bf(ad), wa2_ref[...]) + ba_ref[...]
    gk = (jnp.minimum(z, 0.0) - jnp.log(1.0 + jnp.exp(-jnp.abs(z)))) * (1.0 / GLA_GATE_NORM)

    tri_r = lax.broadcasted_iota(jnp.int32, (c, c), 0)
    tri_c = lax.broadcasted_iota(jnp.int32, (c, c), 1)
    causal = tri_c <= tri_r
    tril = jnp.where(causal, 1.0, 0.0).astype(BF16)
    bcum = _chunk_cumsum(gk, tril)
    blast = bcum[c - 1:c, :]
    q_dec = _bf(q * (dk ** -0.5) * jnp.exp(bcum))
    k_inv = _bf(k * jnp.exp(-bcum))
    k_dec = _bf(k * jnp.exp(blast - bcum))
    chunk_decay = jnp.exp(blast)

    outs = []
    for h in range(heads):
        ks = slice(h * dk, (h + 1) * dk)
        vh = _bf(v[:, h * dv:(h + 1) * dv])
        att = jnp.where(causal, _dot_nt(q_dec[:, ks], k_inv[:, ks]), 0.0)
        st = state_ref[h]
        oh = _dot(_bf(att), vh) + _dot_nt(q_dec[:, ks], _bf(st))
        state_ref[h] = st * chunk_decay[:, ks] + _dot_tn(vh, k_dec[:, ks])
        ms = jnp.mean(oh * oh, axis=-1, keepdims=True)
        outs.append(oh * lax.rsqrt(ms + GLA_NORM_EPS) * ng_ref[...])
    o = jnp.concatenate(outs, axis=1)
    y_ref[...] = (o * (go * _sigmoid(go))).astype(y_ref.dtype)


def _gla_mix(p, wa2, b_a, norm_g, *, batch, seq, heads, dk, dv):
    t, width = p.shape
    adp = wa2.shape[0]
    e = heads * dv
    nc = seq // CHUNK
    blk = _nbytes((CHUNK, width), F32) + _nbytes((CHUNK, e), BF16) + _nbytes(wa2.shape, wa2.dtype) \
        + 16 * _nbytes((CHUNK, e), F32)
    return pl.pallas_call(
        functools.partial(_gla_kernel, heads=heads, dk=dk, dv=dv, adp=adp),
        out_shape=jax.ShapeDtypeStruct((t, e), BF16),
        grid=(batch, nc),
        in_specs=[pl.BlockSpec((CHUNK, width), lambda b, i: (b * nc + i, 0)),
                  pl.BlockSpec(wa2.shape, lambda b, i: (0, 0)),
                  pl.BlockSpec((1, heads * dk), lambda b, i: (0, 0)),
                  pl.BlockSpec((1, dv), lambda b, i: (0, 0))],
        out_specs=pl.BlockSpec((CHUNK, e), lambda b, i: (b * nc + i, 0)),
        scratch_shapes=[pltpu.VMEM((heads, dv, dk), F32)],
        compiler_params=_params(2, blk),
        name="gla_mix",
    )(p, wa2, b_a, norm_g)


def _causal_conv3(u, prev8, w):
    return _shift_rows(u, prev8, 2) * w[0:1] + _shift_rows(u, prev8, 1) * w[1:2] + u * w[2:3]


def _sconv_kernel(bg_ref, cg_ref, hh_ref, w_ref, y_ref, carry_ref):
    @pl.when(pl.program_id(1) == 0)
    def _():
        carry_ref[...] = jnp.zeros_like(carry_ref)

    u = cg_ref[...] * hh_ref[...]
    y = bg_ref[...] * _causal_conv3(u, carry_ref[...], w_ref[...])
    carry_ref[...] = u[u.shape[0] - V7X_SUBLANES:]
    y_ref[...] = y.astype(y_ref.dtype)


def _sconv_mix(p, conv_w, *, batch, seq, e):
    t = p.shape[0]
    tm = _pick_tile(seq, 512, V7X_SUBLANES)
    ms = seq // tm
    col = lambda g: pl.BlockSpec((tm, e), lambda b, i, g=g: (b * ms + i, g))
    blk = 3 * _nbytes((tm, e), F32) + _nbytes((tm, e), BF16) + 4 * _nbytes((tm, e), F32)
    return pl.pallas_call(
        _sconv_kernel,
        out_shape=jax.ShapeDtypeStruct((t, e), BF16),
        grid=(batch, ms),
        in_specs=[col(0), col(1), col(2), pl.BlockSpec((CONV_WIDTH, e), lambda b, i: (0, 0))],
        out_specs=pl.BlockSpec((tm, e), lambda b, i: (b * ms + i, 0)),
        scratch_shapes=[pltpu.VMEM((V7X_SUBLANES, e), F32)],
        compiler_params=_params(2, blk),
        name="short_conv_mix",
    )(p, p, p, conv_w)


def _merge_kernel(ya_ref, yb_ref, yc_ref, wa_ref, wb_ref, wc_ref, ga_ref, gb_ref, gc_ref, o_ref):
    acc = ga_ref[...].astype(F32) * _dot(ya_ref[...], wa_ref[...])
    acc = acc + gb_ref[...].astype(F32) * _dot(yb_ref[...], wb_ref[...])
    acc = acc + gc_ref[...].astype(F32) * _dot(yc_ref[...], wc_ref[...])
    o_ref[...] = acc.astype(o_ref.dtype)


def _merge(ya, yb, yc, w_br, gates):
    t, e = ya.shape
    d = w_br.shape[2]
    tm = _pick_tile(t, 512, V7X_SUBLANES)
    tn = _pick_tile(d, 512)
    nt = d // tn
    ysp = pl.BlockSpec((tm, e), lambda j, i: (i, 0))
    wsp = lambda g: pl.BlockSpec((None, e, tn), lambda j, i, g=g: (g, 0, j))
    gsp = lambda g: pl.BlockSpec((tm, tn), lambda j, i, g=g: (i, g * nt + j))
    blk = 3 * _nbytes((tm, e), BF16) + 3 * _nbytes((e, tn), BF16) + 3 * _nbytes((tm, tn), gates.dtype) \
        + _nbytes((tm, tn), BF16) + 2 * _nbytes((tm, tn), F32)
    return pl.pallas_call(
        _merge_kernel,
        out_shape=jax.ShapeDtypeStruct((t, d), BF16),
        grid=(nt, t // tm),
        in_specs=[ysp, ysp, ysp, wsp(0), wsp(1), wsp(2), gsp(0), gsp(1), gsp(2)],
        out_specs=pl.BlockSpec((tm, tn), lambda j, i: (i, j)),
        compiler_params=_params(2, blk),
        name="gated_merge",
    )(ya, yb, yc, w_br, w_br, w_br, gates, gates, gates)


def _ffn_up_kernel(h_ref, wg_ref, wv_ref, cg_ref, cv_ref, act_ref, carry_ref):
    m = pl.program_id(1)
    n = pl.program_id(2)

    @pl.when(m == 0)
    def _():
        carry_ref[n] = jnp.zeros(carry_ref.shape[1:], F32)

    h = h_ref[...]
    ug = _dot(h, wg_ref[...])
    uv = _dot(h, wv_ref[...])
    tail = ug.shape[0] - V7X_SUBLANES
    prev_g = carry_ref[n, 0]
    prev_v = carry_ref[n, 1]
    carry_ref[n, 0] = ug[tail:]
    carry_ref[n, 1] = uv[tail:]
    xg = _causal_conv3(ug, prev_g, cg_ref[...])
    xv = _causal_conv3(uv, prev_v, cv_ref[...])
    gelu = 0.5 * xg * (1.0 + jnp.tanh(0.7978845608028654 * (xg + 0.044715 * (xg * xg * xg))))
    act_ref[...] = (gelu * xv).astype(act_ref.dtype)


def _ffn_up(h, w_up, conv_w, *, batch, seq):
    t, d = h.shape
    f = w_up.shape[1] // 2
    tm = _pick_tile(seq, 1024, V7X_SUBLANES)
    tn = _pick_tile(f, 256)
    ms, nt = seq // tm, f // tn
    blk = _nbytes((tm, d), BF16) + 2 * _nbytes((d, tn), BF16) + _nbytes((tm, tn), BF16) \
        + 8 * _nbytes((tm, tn), F32)
    return pl.pallas_call(
        _ffn_up_kernel,
        out_shape=jax.ShapeDtypeStruct((t, f), BF16),
        grid=(batch, ms, nt),
        in_specs=[pl.BlockSpec((tm, d), lambda b, i, j: (b * ms + i, 0)),
                  pl.BlockSpec((d, tn), lambda b, i, j: (0, j)),
                  pl.BlockSpec((d, tn), lambda b, i, j: (0, nt + j)),
                  pl.BlockSpec((CONV_WIDTH, tn), lambda b, i, j: (0, j)),
                  pl.BlockSpec((CONV_WIDTH, tn), lambda b, i, j: (0, nt + j))],
        out_specs=pl.BlockSpec((tm, tn), lambda b, i, j: (b * ms + i, j)),
        scratch_shapes=[pltpu.VMEM((nt, 2, V7X_SUBLANES, tn), F32)],
        compiler_params=_params(3, blk),
        name="ffn_up_conv_gate",
    )(h, w_up, w_up, conv_w, conv_w)


def _pad_cols(w, n):
    return jnp.pad(w, ((0, 0), (0, n - w.shape[1])))


def _pad_rows(w, n):
    return jnp.pad(w, ((0, n - w.shape[0]), (0, 0)))


def _layer(x, h, l, prm, next_pre_g, *, batch, seq):
    (w_in, rwkv_mu, rwkv_w0, rwkv_w_w2, rwkv_a0, rwkv_w_a2, rwkv_w_g2, rwkv_k_k, rwkv_k_a, rwkv_r_k,
     rwkv_ln_w, rwkv_ln_b, gla_w_a2, gla_b_a, gla_norm_g, sconv_w, w_br, w_o, post_mix_g, pre_ffn_g,
     w_up, ffn_conv_w, w_down, post_ffn_g) = [a[l] for a in prm]
    d = x.shape[1]
    e = w_br.shape[1]
    lw, la, lg = rwkv_w_w2.shape[0], rwkv_w_a2.shape[0], rwkv_w_g2.shape[0]
    lwp, lap = _round_up(lw, V7X_LANES), _round_up(la, V7X_LANES)
    gr, kw = gla_w_a2.shape
    dv = gla_norm_g.shape[0]
    heads = e // dv
    dk = kw // heads
    adp = _round_up(gr, 2 * V7X_LANES)
    row = lambda a: a.reshape(1, -1)

    o1 = 3 * e + lw + la + lg
    o2 = o1 + 2 * kw + e + gr + e
    o3 = o2 + 3 * e
    c0 = 3 * e
    w_rwkv = jnp.concatenate(
        [w_in[:, :c0], _pad_cols(w_in[:, c0:c0 + lw], lwp), _pad_cols(w_in[:, c0 + lw:c0 + lw + la], lap),
         w_in[:, c0 + lw + la:o1]], axis=1).astype(BF16)
    mu = rwkv_mu
    mu_p = jnp.concatenate(
        [mu[:c0], jnp.pad(mu[c0:c0 + lw], (0, lwp - lw)), jnp.pad(mu[c0 + lw:c0 + lw + la], (0, lap - la)),
         mu[c0 + lw + la:]]).reshape(1, -1)
    g0 = o1 + 2 * kw + e
    w_gla = jnp.concatenate(
        [w_in[:, o1:g0], _pad_cols(w_in[:, g0:g0 + gr], adp), w_in[:, g0 + gr:o2]], axis=1).astype(BF16)
    w_conv = w_in[:, o2:o3].astype(BF16)
    w_gate = w_in[:, o3:].astype(BF16)

    p_rwkv = _matmul(h, w_rwkv, out_dtype=F32, tn=512, name="proj_rwkv")
    p_gla = _matmul(h, w_gla, out_dtype=F32, tn=1280, name="proj_gla")
    p_conv = _matmul(h, w_conv, out_dtype=F32, tn=1024, name="proj_conv")
    gates = _matmul(h, w_gate, out_dtype=BF16, act="sigmoid", tn=1024, name="proj_gate")

    y_a = _rwkv_mix(p_rwkv, mu_p, row(rwkv_w0), _pad_rows(rwkv_w_w2, lwp).astype(BF16), row(rwkv_a0),
                    _pad_rows(rwkv_w_a2, lap).astype(BF16), rwkv_w_g2.astype(BF16), row(rwkv_k_k),
                    row(rwkv_k_a), row(rwkv_r_k), row(rwkv_ln_w), row(rwkv_ln_b),
                    batch=batch, seq=seq, e=e)
    y_b = _gla_mix(p_gla, _pad_rows(gla_w_a2, adp).astype(BF16), row(gla_b_a), row(gla_norm_g),
                   batch=batch, seq=seq, heads=heads, dk=dk, dv=dv)
    y_c = _sconv_mix(p_conv, sconv_w, batch=batch, seq=seq, e=e)

    merged = _merge(y_a, y_b, y_c, w_br.astype(BF16), gates)
    o = _matmul(merged, w_o.astype(BF16), out_dtype=F32, name="proj_out")
    x, h2 = _residual_norm(o, x, post_mix_g, pre_ffn_g)

    act = _ffn_up(h2, w_up.astype(BF16), ffn_conv_w, batch=batch, seq=seq)
    o2 = _matmul(act, w_down.astype(BF16), out_dtype=F32, tm=512, tn=512, weight_stationary=False,
                 name="ffn_down")
    return _residual_norm(o2, x, post_ffn_g, next_pre_g)


def kernel(x, pre_mix_g, w_in, rwkv_mu, rwkv_w0, rwkv_w_w2, rwkv_a0, rwkv_w_a2, rwkv_w_g2, rwkv_k_k, rwkv_k_a, rwkv_r_k, rwkv_ln_w, rwkv_ln_b, gla_w_a2, gla_b_a, gla_norm_g, sconv_w, w_br, w_o, post_mix_g, pre_ffn_g, w_up, ffn_conv_w, w_down, post_ffn_g):
    batch, seq, d = x.shape
    depth = w_in.shape[0]
    prm = (w_in, rwkv_mu, rwkv_w0, rwkv_w_w2, rwkv_a0, rwkv_w_a2, rwkv_w_g2, rwkv_k_k, rwkv_k_a,
           rwkv_r_k.reshape(depth, -1), rwkv_ln_w, rwkv_ln_b, gla_w_a2, gla_b_a, gla_norm_g, sconv_w,
           w_br, w_o, post_mix_g, pre_ffn_g, w_up, ffn_conv_w, w_down, post_ffn_g)
    xf = x.reshape(batch * seq, d)
    h = _rms_norm_bf16(xf, pre_mix_g[0])
    for l in range(depth):
        next_g = pre_mix_g[l + 1] if l + 1 < depth else None
        xf, h = _layer(xf, h, l, prm, next_g, batch=batch, seq=seq)
    return xf.reshape(batch, seq, d)
```

```python
import functools

import jax
import jax.numpy as jnp
from jax import lax
from jax.experimental import pallas as pl
from jax.experimental.pallas import tpu as pltpu

F32 = jnp.float32
BF16 = jnp.bfloat16

NORM_EPS = 1e-6
RWKV_HEAD_DIM = 64
RWKV_GN_EPS = 1e-5 * RWKV_HEAD_DIM
GLA_GATE_NORM = 16.0
GLA_NORM_EPS = 1e-5
CONV_WIDTH = 3
CHUNK = 64

V7X_LANES = 128
V7X_SUBLANES = 8
V7X_VMEM_BYTES = 64 * 1024 * 1024
VMEM_CAP_BYTES = V7X_VMEM_BYTES - 6 * 1024 * 1024


def _round_up(n, m):
    return (n + m - 1) // m * m


def _pick_tile(n, pref, quantum=V7X_LANES):
    if n <= pref:
        return n
    t = pref - pref % quantum
    while t >= quantum:
        if n % t == 0:
            return t
        t -= quantum
    return n


def _params(n_axes, block_bytes):
    limit = min(VMEM_CAP_BYTES, 2 * block_bytes + 16 * 1024 * 1024)
    return pltpu.CompilerParams(
        dimension_semantics=("arbitrary",) * n_axes, vmem_limit_bytes=int(limit))


def _nbytes(shape, dtype):
    n = 1
    for s in shape:
        n *= s
    return n * jnp.dtype(dtype).itemsize


def _dot(a, b):
    return jnp.dot(a, b, preferred_element_type=F32)


def _dot_nt(a, b):
    return lax.dot_general(a, b, (((1,), (1,)), ((), ())), preferred_element_type=F32)


def _dot_tn(a, b):
    return lax.dot_general(a, b, (((0,), (0,)), ((), ())), preferred_element_type=F32)


def _bf(x):
    return x.astype(BF16)


def _sigmoid(x):
    return 1.0 / (1.0 + jnp.exp(-x))


def _split3(x):
    hi = _bf(x)
    r1 = x - hi.astype(F32)
    mid = _bf(r1)
    lo = _bf(r1 - mid.astype(F32))
    return hi, mid, lo


def _chunk_cumsum(x, tril_bf):
    hi, mid, lo = _split3(x)
    return _dot(tril_bf, hi) + _dot(tril_bf, mid) + _dot(tril_bf, lo)


def _shift_rows(u, prev8, k):
    rolled = pltpu.roll(u, k, 0)
    rolled_prev = pltpu.roll(prev8, k, 0)
    row = lax.broadcasted_iota(jnp.int32, prev8.shape, 0)
    top = jnp.where(row < k, rolled_prev, rolled[:V7X_SUBLANES])
    return jnp.concatenate([top, rolled[V7X_SUBLANES:]], axis=0)


def _rms_kernel(x_ref, g_ref, h_ref):
    x = x_ref[...]
    ms = jnp.mean(x * x, axis=-1, keepdims=True)
    h_ref[...] = (x * lax.rsqrt(ms + NORM_EPS) * g_ref[...]).astype(h_ref.dtype)


def _rms_norm_bf16(x, g):
    t, d = x.shape
    tm = _pick_tile(t, 256, V7X_SUBLANES)
    blk = _nbytes((tm, d), F32) + _nbytes((tm, d), BF16)
    return pl.pallas_call(
        _rms_kernel,
        out_shape=jax.ShapeDtypeStruct((t, d), BF16),
        grid=(t // tm,),
        in_specs=[pl.BlockSpec((tm, d), lambda i: (i, 0)),
                  pl.BlockSpec((1, d), lambda i: (0, 0))],
        out_specs=pl.BlockSpec((tm, d), lambda i: (i, 0)),
        compiler_params=_params(1, blk),
        name="rms_norm",
    )(x, g.reshape(1, d))


def _residual_kernel(o_ref, x_ref, gpost_ref, *rest, with_next):
    o = o_ref[...]
    ms = jnp.mean(o * o, axis=-1, keepdims=True)
    xn = x_ref[...] + o * lax.rsqrt(ms + NORM_EPS) * gpost_ref[...]
    if with_next:
        gpre_ref, xo_ref, h_ref = rest
        ms2 = jnp.mean(xn * xn, axis=-1, keepdims=True)
        h_ref[...] = (xn * lax.rsqrt(ms2 + NORM_EPS) * gpre_ref[...]).astype(h_ref.dtype)
    else:
        (xo_ref,) = rest
    xo_ref[...] = xn


def _residual_norm(o, x, g_post, g_next):
    t, d = x.shape
    tm = _pick_tile(t, 256, V7X_SUBLANES)
    with_next = g_next is not None
    row = pl.BlockSpec((tm, d), lambda i: (i, 0))
    vec = pl.BlockSpec((1, d), lambda i: (0, 0))
    in_specs = [row, row, vec] + ([vec] if with_next else [])
    args = [o, x, g_post.reshape(1, d)] + ([g_next.reshape(1, d)] if with_next else [])
    out_shape = [jax.ShapeDtypeStruct((t, d), F32)]
    out_specs = [row]
    if with_next:
        out_shape.append(jax.ShapeDtypeStruct((t, d), BF16))
        out_specs.append(row)
    blk = 3 * _nbytes((tm, d), F32) + _nbytes((tm, d), BF16)
    res = pl.pallas_call(
        functools.partial(_residual_kernel, with_next=with_next),
        out_shape=out_shape,
        grid=(t // tm,),
        in_specs=in_specs,
        out_specs=out_specs,
        compiler_params=_params(1, blk),
        name="residual_norm",
    )(*args)
    return (res[0], res[1]) if with_next else (res[0], None)


def _mm_kernel(x_ref, w_ref, o_ref, *, act):
    acc = _dot(x_ref[...], w_ref[...])
    if act == "sigmoid":
        acc = _sigmoid(acc)
    o_ref[...] = acc.astype(o_ref.dtype)


def _matmul(x, w, *, out_dtype, act=None, tm=1024, tn=1024, weight_stationary=True, name="matmul"):
    m, k = x.shape
    _, n = w.shape
    tm = _pick_tile(m, tm, V7X_SUBLANES)
    tn = _pick_tile(n, tn)
    if weight_stationary:
        grid = (n // tn, m // tm)
        xm, wm, om = (lambda j, i: (i, 0)), (lambda j, i: (0, j)), (lambda j, i: (i, j))
    else:
        grid = (m // tm, n // tn)
        xm, wm, om = (lambda i, j: (i, 0)), (lambda i, j: (0, j)), (lambda i, j: (i, j))
    blk = _nbytes((tm, k), x.dtype) + _nbytes((k, tn), w.dtype) + _nbytes((tm, tn), out_dtype) \
        + _nbytes((tm, tn), F32)
    return pl.pallas_call(
        functools.partial(_mm_kernel, act=act),
        out_shape=jax.ShapeDtypeStruct((m, n), out_dtype),
        grid=grid,
        in_specs=[pl.BlockSpec((tm, k), xm), pl.BlockSpec((k, tn), wm)],
        out_specs=pl.BlockSpec((tm, tn), om),
        compiler_params=_params(2, blk),
        name=name,
    )(x, w)


def _rwkv_kernel(p_ref, mu_ref, w0_ref, ww2_ref, a0_ref, wa2_ref, wg2_ref, kk_ref, ka_ref, rk_ref,
                 lnw_ref, lnb_ref, y_ref, prev_ref, state_ref, *, e, lwp, lap):
    c = p_ref.shape[0]
    pair = 2 * RWKV_HEAD_DIM
    n_pairs = e // pair

    @pl.when(pl.program_id(1) == 0)
    def _():
        prev_ref[...] = jnp.zeros_like(prev_ref)
        state_ref[...] = jnp.zeros_like(state_ref)

    p = p_ref[...]
    shifted = _shift_rows(p, prev_ref[...], 1)
    prev_ref[...] = p[c - V7X_SUBLANES:]
    xm = p + (shifted - p) * mu_ref[...]

    r = xm[:, 0:e]
    k = xm[:, e:2 * e]
    v = xm[:, 2 * e:3 * e]
    o = 3 * e
    wd = xm[:, o:o + lwp]
    ad = xm[:, o + lwp:o + lwp + lap]
    gd = xm[:, o + lwp + lap:]

    w_pre = w0_ref[...] + _dot(_bf(jnp.tanh(wd)), ww2_ref[...])
    lw = (-0.6065306597126334) * _sigmoid(w_pre)
    a = _sigmoid(a0_ref[...] + _dot(_bf(ad), wa2_ref[...]))
    g = _dot(_bf(_sigmoid(gd)), wg2_ref[...])

    ri = lax.broadcasted_iota(jnp.int32, (pair, pair), 0)
    ci = lax.broadcasted_iota(jnp.int32, (pair, pair), 1)
    same_head = (ri < RWKV_HEAD_DIM) == (ci < RWKV_HEAD_DIM)
    head_ones = jnp.where(same_head, 1.0, 0.0).astype(BF16)

    def head_sum(x):
        hi = _bf(x)
        lo = _bf(x - hi.astype(F32))
        xs = jnp.concatenate([hi, lo], axis=0)
        s = jnp.concatenate(
            [_dot(xs[:, j * pair:(j + 1) * pair], head_ones) for j in range(n_pairs)], axis=1)
        return s[:c] + s[c:]

    kk = k * kk_ref[...]
    kk = kk / jnp.maximum(jnp.sqrt(head_sum(kk * kk)), 1e-12)
    kmod = k * (1.0 + (a - 1.0) * ka_ref[...])
    bvec = kk * a

    tri_r = lax.broadcasted_iota(jnp.int32, (c, c), 0)
    tri_c = lax.broadcasted_iota(jnp.int32, (c, c), 1)
    tril = jnp.where(tri_c <= tri_r, 1.0, 0.0).astype(BF16)
    gcum = _chunk_cumsum(lw, tril)
    glast = gcum[c - 1:c, :]
    e_pos = jnp.exp(gcum)
    e_neg = jnp.exp(-gcum)
    e_tail = jnp.exp(glast - gcum)
    r_t = r * e_pos
    kap_t = kk * jnp.exp(gcum - lw)
    b_t = bvec * e_neg
    k_t = kmod * e_neg
    b_dec = bvec * e_tail
    k_dec = kmod * e_tail
    gamma_last = jnp.exp(glast)

    wt = lax.broadcasted_iota(jnp.int32, (c, pair), 0)
    wl = lax.broadcasted_iota(jnp.int32, (c, pair), 1)
    ws = wl & (RWKV_HEAD_DIM - 1)
    lane_lo = wl < RWKV_HEAD_DIM
    m_strict = ws < wt
    m_incl = ws <= wt
    m_16 = m_strict & ((ws >> 4) == (wt >> 4))
    m_32 = m_strict & ((ws >> 5) == (wt >> 5)) & ((ws >> 4) != (wt >> 4))
    m_64 = m_strict & ((ws >> 5) != (wt >> 5))
    eye_w = jnp.where(ws == wt, 1.0, 0.0).astype(F32)
    zero_b = jnp.zeros((c, pair), BF16)

    def stack2(x):
        xb = _bf(x)
        return jnp.concatenate([jnp.where(lane_lo, xb, zero_b), jnp.where(lane_lo, zero_b, xb)], axis=0)

    pairs = range(n_pairs)
    cols = [slice(j * pair, (j + 1) * pair) for j in pairs]
    lhs = [_bf(jnp.concatenate([kap_t[:, s], r_t[:, s]], axis=0)) for s in cols]
    bk_s = [jnp.concatenate([stack2(b_t[:, s]), stack2(k_t[:, s])], axis=0) for s in cols]
    v_s = [stack2(v[:, s]) for s in cols]
    scores = [_dot_nt(lhs[j], bk_s[j]) for j in pairs]
    st = [state_ref[j] for j in pairs]
    proj = [_dot_nt(lhs[j], _bf(st[j])) for j in pairs]
    a_w = [jnp.where(m_strict, scores[j][:c, :pair], 0.0) for j in pairs]
    b_w = [_bf(jnp.where(m_strict, scores[j][:c, pair:], 0.0)) for j in pairs]
    r_w = [_bf(jnp.concatenate([jnp.where(m_incl, scores[j][c:, :pair], 0.0),
                                jnp.where(m_incl, scores[j][c:, pair:], 0.0)], axis=1)) for j in pairs]
    rhs_u = [-(proj[j][:c] + _dot(b_w[j], v_s[j])) for j in pairs]

    d1 = [jnp.where(m_16, a_w[j], 0.0) for j in pairs]
    e32 = [stack2(jnp.where(m_32, a_w[j], 0.0)) for j in pairs]
    e64 = [_bf(jnp.where(m_64, a_w[j], 0.0)) for j in pairs]
    t_inv = [eye_w - d1[j] for j in pairs]
    d2 = [_dot(_bf(d1[j]), stack2(d1[j])) for j in pairs]
    nxt = [_dot(_bf(jnp.concatenate([t_inv[j], d2[j]], axis=0)), stack2(d2[j])) for j in pairs]
    t_inv = [t_inv[j] + nxt[j][:c] for j in pairs]
    d4 = [nxt[j][c:] for j in pairs]
    nxt = [_dot(_bf(jnp.concatenate([t_inv[j], d4[j]], axis=0)), stack2(d4[j])) for j in pairs]
    t_inv = [t_inv[j] + nxt[j][:c] for j in pairs]
    t_inv = [t_inv[j] + _dot(_bf(t_inv[j]), stack2(nxt[j][c:])) for j in pairs]
    te = [_dot(_bf(t_inv[j]), e32[j]) for j in pairs]
    t32 = [_bf(t_inv[j] - _dot(_bf(te[j]), stack2(t_inv[j]))) for j in pairs]
    z = [_dot(t32[j], stack2(rhs_u[j])) for j in pairs]
    ez = [_dot(e64[j], stack2(z[j])) for j in pairs]
    u = [z[j] - _dot(t32[j], stack2(ez[j])) for j in pairs]

    uv_s = [jnp.concatenate([stack2(u[j]), v_s[j]], axis=0) for j in pairs]
    y_cols = [proj[j][c:] + _dot(r_w[j], uv_s[j]) for j in pairs]

    rj = lax.broadcasted_iota(jnp.int32, (pair, pair), 0)
    cj = lax.broadcasted_iota(jnp.int32, (pair, pair), 1)
    same_head_sq = (rj < RWKV_HEAD_DIM) == (cj < RWKV_HEAD_DIM)
    for j in pairs:
        s = cols[j]
        uv = _bf(jnp.concatenate([u[j], v[:, s]], axis=0))
        bk = _bf(jnp.concatenate([b_dec[:, s], k_dec[:, s]], axis=0))
        ds = jnp.where(same_head_sq, _dot_tn(uv, bk), 0.0)
        state_ref[j] = st[j] * gamma_last[:, s] + ds

    y = jnp.concatenate(y_cols, axis=1)
    inv_n = 1.0 / RWKV_HEAD_DIM
    mean = head_sum(y) * inv_n
    yc = y - mean
    var = head_sum(yc * yc) * inv_n
    yn = yc * lax.rsqrt(var + RWKV_GN_EPS) * lnw_ref[...] + lnb_ref[...]
    bonus = head_sum(r * kmod * rk_ref[...]) * v
    y_ref[...] = ((yn + bonus) * g).astype(y_ref.dtype)


def _rwkv_mix(p, mu, w0, ww2, a0, wa2, wg2, k_k, k_a, r_k, ln_w, ln_b, *, batch, seq, e):
    t, width = p.shape
    lwp, lap = ww2.shape[0], wa2.shape[0]
    assert CHUNK == RWKV_HEAD_DIM and e % (2 * RWKV_HEAD_DIM) == 0 and seq % CHUNK == 0
    nc = seq // CHUNK
    n_pairs = e // (2 * RWKV_HEAD_DIM)
    vec = lambda n: pl.BlockSpec((1, n), lambda b, i: (0, 0))
    full = lambda a: pl.BlockSpec(a.shape, lambda b, i: (0, 0))
    blk = _nbytes((CHUNK, width), F32) + _nbytes((CHUNK, e), BF16) \
        + sum(_nbytes(a.shape, a.dtype) for a in (ww2, wa2, wg2)) + 24 * _nbytes((CHUNK, e), F32)
    return pl.pallas_call(
        functools.partial(_rwkv_kernel, e=e, lwp=lwp, lap=lap),
        out_shape=jax.ShapeDtypeStruct((t, e), BF16),
        grid=(batch, nc),
        in_specs=[pl.BlockSpec((CHUNK, width), lambda b, i: (b * nc + i, 0)),
                  vec(width), vec(e), full(ww2), vec(e), full(wa2), full(wg2),
                  vec(e), vec(e), vec(e), vec(e), vec(e)],
        out_specs=pl.BlockSpec((CHUNK, e), lambda b, i: (b * nc + i, 0)),
        scratch_shapes=[pltpu.VMEM((V7X_SUBLANES, width), F32),
                        pltpu.VMEM((n_pairs, 2 * RWKV_HEAD_DIM, 2 * RWKV_HEAD_DIM), F32)],
        compiler_params=_params(2, blk),
        name="rwkv7_mix",
    )(p, mu, w0, ww2, a0, wa2, wg2, k_k, k_a, r_k, ln_w, ln_b)


def _gla_kernel(p_ref, wa2_ref, ba_ref, ng_ref, y_ref, state_ref, *, heads, dk, dv, adp):
    c = p_ref.shape[0]
    kw = heads * dk
    e = heads * dv

    @pl.when(pl.program_id(1) == 0)
    def _():
        state_ref[...] = jnp.zeros_like(state_ref)

    p = p_ref[...]
    q = p[:, 0:kw]
    k = p[:, kw:2 * kw]
    v = p[:, 2 * kw:2 * kw + e]
    ad = p[:, 2 * kw + e:2 * kw + e + adp]
    go = p[:, 2 * kw + e + adp:]

    z = _dot(_bf(ad), wa2_ref[...]) + ba_ref[...]
    gk = (jnp.minimum(z, 0.0) - jnp.log(1.0 + jnp.exp(-jnp.abs(z)))) * (1.0 / GLA_GATE_NORM)

    tri_r = lax.broadcasted_iota(jnp.int32, (c, c), 0)
    tri_c = lax.broadcasted_iota(jnp.int32, (c, c), 1)
    causal = tri_c <= tri_r
    tril = jnp.where(causal, 1.0, 0.0).astype(BF16)
    bcum = _chunk_cumsum(gk, tril)
    blast = bcum[c - 1:c, :]
    q_dec = _bf(q * (dk ** -0.5) * jnp.exp(bcum))
    k_inv = _bf(k * jnp.exp(-bcum))
    k_dec = _bf(k * jnp.exp(blast - bcum))
    chunk_decay = jnp.exp(blast)

    outs = []
    for h in range(heads):
        ks = slice(h * dk, (h + 1) * dk)
        vh = _bf(v[:, h * dv:(h + 1) * dv])
        att = jnp.where(causal, _dot_nt(q_dec[:, ks], k_inv[:, ks]), 0.0)
        st = state_ref[h]
        oh = _dot(_bf(att), vh) + _dot_nt(q_dec[:, ks], _bf(st))
        state_ref[h] = st * chunk_decay[:, ks] + _dot_tn(vh, k_dec[:, ks])
        ms = jnp.mean(oh * oh, axis=-1, keepdims=True)
        outs.append(oh * lax.rsqrt(ms + GLA_NORM_EPS) * ng_ref[...])
    o = jnp.concatenate(outs, axis=1)
    y_ref[...] = (o * (go * _sigmoid(go))).astype(y_ref.dtype)


def _gla_mix(p, wa2, b_a, norm_g, *, batch, seq, heads, dk, dv):
    t, width = p.shape
    adp = wa2.shape[0]
    e = heads * dv
    nc = seq // CHUNK
    blk = _nbytes((CHUNK, width), F32) + _nbytes((CHUNK, e), BF16) + _nbytes(wa2.shape, wa2.dtype) \
        + 16 * _nbytes((CHUNK, e), F32)
    return pl.pallas_call(
        functools.partial(_gla_kernel, heads=heads, dk=dk, dv=dv, adp=adp),
        out_shape=jax.ShapeDtypeStruct((t, e), BF16),
        grid=(batch, nc),
        in_specs=[pl.BlockSpec((CHUNK, width), lambda b, i: (b * nc + i, 0)),
                  pl.BlockSpec(wa2.shape, lambda b, i: (0, 0)),
                  pl.BlockSpec((1, heads * dk), lambda b, i: (0, 0)),
                  pl.BlockSpec((1, dv), lambda b, i: (0, 0))],
        out_specs=pl.BlockSpec((CHUNK, e), lambda b, i: (b * nc + i, 0)),
        scratch_shapes=[pltpu.VMEM((heads, dv, dk), F32)],
        compiler_params=_params(2, blk),
        name="gla_mix",
    )(p, wa2, b_a, norm_g)


def _causal_conv3(u, prev8, w):
    return _shift_rows(u, prev8, 2) * w[0:1] + _shift_rows(u, prev8, 1) * w[1:2] + u * w[2:3]


def _sconv_kernel(bg_ref, cg_ref, hh_ref, w_ref, y_ref, carry_ref):
    @pl.when(pl.program_id(1) == 0)
    def _():
        carry_ref[...] = jnp.zeros_like(carry_ref)

    u = cg_ref[...] * hh_ref[...]
    y = bg_ref[...] * _causal_conv3(u, carry_ref[...], w_ref[...])
    carry_ref[...] = u[u.shape[0] - V7X_SUBLANES:]
    y_ref[...] = y.astype(y_ref.dtype)


def _sconv_mix(p, conv_w, *, batch, seq, e):
    t = p.shape[0]
    tm = _pick_tile(seq, 512, V7X_SUBLANES)
    ms = seq // tm
    col = lambda g: pl.BlockSpec((tm, e), lambda b, i, g=g: (b * ms + i, g))
    blk = 3 * _nbytes((tm, e), F32) + _nbytes((tm, e), BF16) + 4 * _nbytes((tm, e), F32)
    return pl.pallas_call(
        _sconv_kernel,
        out_shape=jax.ShapeDtypeStruct((t, e), BF16),
        grid=(batch, ms),
        in_specs=[col(0), col(1), col(2), pl.BlockSpec((CONV_WIDTH, e), lambda b, i: (0, 0))],
        out_specs=pl.BlockSpec((tm, e), lambda b, i: (b * ms + i, 0)),
        scratch_shapes=[pltpu.VMEM((V7X_SUBLANES, e), F32)],
        compiler_params=_params(2, blk),
        name="short_conv_mix",
    )(p, p, p, conv_w)


def _merge_kernel(ya_ref, yb_ref, yc_ref, wa_ref, wb_ref, wc_ref, ga_ref, gb_ref, gc_ref, o_ref):
    acc = ga_ref[...].astype(F32) * _dot(ya_ref[...], wa_ref[...])
    acc = acc + gb_ref[...].astype(F32) * _dot(yb_ref[...], wb_ref[...])
    acc = acc + gc_ref[...].astype(F32) * _dot(yc_ref[...], wc_ref[...])
    o_ref[...] = acc.astype(o_ref.dtype)


def _merge(ya, yb, yc, w_br, gates):
    t, e = ya.shape
    d = w_br.shape[2]
    tm = _pick_tile(t, 512, V7X_SUBLANES)
    tn = _pick_tile(d, 512)
    nt = d // tn
    ysp = pl.BlockSpec((tm, e), lambda j, i: (i, 0))
    wsp = lambda g: pl.BlockSpec((None, e, tn), lambda j, i, g=g: (g, 0, j))
    gsp = lambda g: pl.BlockSpec((tm, tn), lambda j, i, g=g: (i, g * nt + j))
    blk = 3 * _nbytes((tm, e), BF16) + 3 * _nbytes((e, tn), BF16) + 3 * _nbytes((tm, tn), gates.dtype) \
        + _nbytes((tm, tn), BF16) + 2 * _nbytes((tm, tn), F32)
    return pl.pallas_call(
        _merge_kernel,
        out_shape=jax.ShapeDtypeStruct((t, d), BF16),
        grid=(nt, t // tm),
        in_specs=[ysp, ysp, ysp, wsp(0), wsp(1), wsp(2), gsp(0), gsp(1), gsp(2)],
        out_specs=pl.BlockSpec((tm, tn), lambda j, i: (i, j)),
        compiler_params=_params(2, blk),
        name="gated_merge",
    )(ya, yb, yc, w_br, w_br, w_br, gates, gates, gates)


def _ffn_up_kernel(h_ref, wg_ref, wv_ref, cg_ref, cv_ref, act_ref, carry_ref):
    m = pl.program_id(1)
    n = pl.program_id(2)

    @pl.when(m == 0)
    def _():
        carry_ref[n] = jnp.zeros(carry_ref.shape[1:], F32)

    h = h_ref[...]
    ug = _dot(h, wg_ref[...])
    uv = _dot(h, wv_ref[...])
    tail = ug.shape[0] - V7X_SUBLANES
    prev_g = carry_ref[n, 0]
    prev_v = carry_ref[n, 1]
    carry_ref[n, 0] = ug[tail:]
    carry_ref[n, 1] = uv[tail:]
    xg = _causal_conv3(ug, prev_g, cg_ref[...])
    xv = _causal_conv3(uv, prev_v, cv_ref[...])
    gelu = 0.5 * xg * (1.0 + jnp.tanh(0.7978845608028654 * (xg + 0.044715 * (xg * xg * xg))))
    act_ref[...] = (gelu * xv).astype(act_ref.dtype)


def _ffn_up(h, w_up, conv_w, *, batch, seq):
    t, d = h.shape
    f = w_up.shape[1] // 2
    tm = _pick_tile(seq, 1024, V7X_SUBLANES)
    tn = _pick_tile(f, 256)
    ms, nt = seq // tm, f // tn
    blk = _nbytes((tm, d), BF16) + 2 * _nbytes((d, tn), BF16) + _nbytes((tm, tn), BF16) \
        + 8 * _nbytes((tm, tn), F32)
    return pl.pallas_call(
        _ffn_up_kernel,
        out_shape=jax.ShapeDtypeStruct((t, f), BF16),
        grid=(batch, ms, nt),
        in_specs=[pl.BlockSpec((tm, d), lambda b, i, j: (b * ms + i, 0)),
                  pl.BlockSpec((d, tn), lambda b, i, j: (0, j)),
                  pl.BlockSpec((d, tn), lambda b, i, j: (0, nt + j)),
                  pl.BlockSpec((CONV_WIDTH, tn), lambda b, i, j: (0, j)),
                  pl.BlockSpec((CONV_WIDTH, tn), lambda b, i, j: (0, nt + j))],
        out_specs=pl.BlockSpec((tm, tn), lambda b, i, j: (b * ms + i, j)),
        scratch_shapes=[pltpu.VMEM((nt, 2, V7X_SUBLANES, tn), F32)],
        compiler_params=_params(3, blk),
        name="ffn_up_conv_gate",
    )(h, w_up, w_up, conv_w, conv_w)


def _pad_cols(w, n):
    return jnp.pad(w, ((0, 0), (0, n - w.shape[1])))


def _pad_rows(w, n):
    return jnp.pad(w, ((0, n - w.shape[0]), (0, 0)))


def _layer(x, h, l, prm, next_pre_g, *, batch, seq):
    (w_in, rwkv_mu, rwkv_w0, rwkv_w_w2, rwkv_a0, rwkv_w_a2, rwkv_w_g2, rwkv_k_k, rwkv_k_a, rwkv_r_k,
     rwkv_ln_w, rwkv_ln_b, gla_w_a2, gla_b_a, gla_norm_g, sconv_w, w_br, w_o, post_mix_g, pre_ffn_g,
     w_up, ffn_conv_w, w_down, post_ffn_g) = [a[l] for a in prm]
    d = x.shape[1]
    e = w_br.shape[1]
    lw, la, lg = rwkv_w_w2.shape[0], rwkv_w_a2.shape[0], rwkv_w_g2.shape[0]
    lwp, lap = _round_up(lw, V7X_LANES), _round_up(la, V7X_LANES)
    gr, kw = gla_w_a2.shape
    dv = gla_norm_g.shape[0]
    heads = e // dv
    dk = kw // heads
    adp = _round_up(gr, 2 * V7X_LANES)
    row = lambda a: a.reshape(1, -1)

    o1 = 3 * e + lw + la + lg
    o2 = o1 + 2 * kw + e + gr + e
    o3 = o2 + 3 * e
    c0 = 3 * e
    w_rwkv = jnp.concatenate(
        [w_in[:, :c0], _pad_cols(w_in[:, c0:c0 + lw], lwp), _pad_cols(w_in[:, c0 + lw:c0 + lw + la], lap),
         w_in[:, c0 + lw + la:o1]], axis=1).astype(BF16)
    mu = rwkv_mu
    mu_p = jnp.concatenate(
        [mu[:c0], jnp.pad(mu[c0:c0 + lw], (0, lwp - lw)), jnp.pad(mu[c0 + lw:c0 + lw + la], (0, lap - la)),
         mu[c0 + lw + la:]]).reshape(1, -1)
    g0 = o1 + 2 * kw + e
    w_gla = jnp.concatenate(
        [w_in[:, o1:g0], _pad_cols(w_in[:, g0:g0 + gr], adp), w_in[:, g0 + gr:o2]], axis=1).astype(BF16)
    w_conv = w_in[:, o2:o3].astype(BF16)
    w_gate = w_in[:, o3:].astype(BF16)

    p_rwkv = _matmul(h, w_rwkv, out_dtype=F32, tn=512, name="proj_rwkv")
    p_gla = _matmul(h, w_gla, out_dtype=F32, tn=1280, name="proj_gla")
    p_conv = _matmul(h, w_conv, out_dtype=F32, tn=1024, name="proj_conv")
    gates = _matmul(h, w_gate, out_dtype=BF16, act="sigmoid", tn=1024, name="proj_gate")

    y_a = _rwkv_mix(p_rwkv, mu_p, row(rwkv_w0), _pad_rows(rwkv_w_w2, lwp).astype(BF16), row(rwkv_a0),
                    _pad_rows(rwkv_w_a2, lap).astype(BF16), rwkv_w_g2.astype(BF16), row(rwkv_k_k),
                    row(rwkv_k_a), row(rwkv_r_k), row(rwkv_ln_w), row(rwkv_ln_b),
                    batch=batch, seq=seq, e=e)
    y_b = _gla_mix(p_gla, _pad_rows(gla_w_a2, adp).astype(BF16), row(gla_b_a), row(gla_norm_g),
                   batch=batch, seq=seq, heads=heads, dk=dk, dv=dv)
    y_c = _sconv_mix(p_conv, sconv_w, batch=batch, seq=seq, e=e)

    merged = _merge(y_a, y_b, y_c, w_br.astype(BF16), gates)
    o = _matmul(merged, w_o.astype(BF16), out_dtype=F32, name="proj_out")
    x, h2 = _residual_norm(o, x, post_mix_g, pre_ffn_g)

    act = _ffn_up(h2, w_up.astype(BF16), ffn_conv_w, batch=batch, seq=seq)
    o2 = _matmul(act, w_down.astype(BF16), out_dtype=F32, tm=512, tn=512, weight_stationary=False,
                 name="ffn_down")
    return _residual_norm(o2, x, post_ffn_g, next_pre_g)


def kernel(x, pre_mix_g, w_in, rwkv_mu, rwkv_w0, rwkv_w_w2, rwkv_a0, rwkv_w_a2, rwkv_w_g2, rwkv_k_k, rwkv_k_a, rwkv_r_k, rwkv_ln_w, rwkv_ln_b, gla_w_a2, gla_b_a, gla_norm_g, sconv_w, w_br, w_o, post_mix_g, pre_ffn_g, w_up, ffn_conv_w, w_down, post_ffn_g):
    batch, seq, d = x.shape
    depth = w_in.shape[0]
    prm = (w_in, rwkv_mu, rwkv_w0, rwkv_w_w2, rwkv_a0, rwkv_w_a2, rwkv_w_g2, rwkv_k_k, rwkv_k_a,
           rwkv_r_k.reshape(depth, -1), rwkv_ln_w, rwkv_ln_b, gla_w_a2, gla_b_a, gla_norm_g, sconv_w,
           w_br, w_o, post_mix_g, pre_ffn_g, w_up, ffn_conv_w, w_down, post_ffn_g)
    xf = x.reshape(batch * seq, d)
    h = _rms_norm_bf16(xf, pre_mix_g[0])
    for l in range(depth):
        next_g = pre_mix_g[l + 1] if l + 1 < depth else None
        xf, h = _layer(xf, h, l, prm, next_g, batch=batch, seq=seq)
    return xf.reshape(batch, seq, d)
```

```python
import functools

import jax
import jax.numpy as jnp
from jax import lax
from jax.experimental import pallas as pl
from jax.experimental.pallas import tpu as pltpu

F32 = jnp.float32
BF16 = jnp.bfloat16

NORM_EPS = 1e-6
RWKV_HEAD_DIM = 64
RWKV_GN_EPS = 1e-5 * RWKV_HEAD_DIM
GLA_GATE_NORM = 16.0
GLA_NORM_EPS = 1e-5
CONV_WIDTH = 3
CHUNK = 64

V7X_LANES = 128
V7X_SUBLANES = 8
V7X_VMEM_BYTES = 64 * 1024 * 1024
VMEM_CAP_BYTES = V7X_VMEM_BYTES - 6 * 1024 * 1024

PROJ_TN = 512
PROJ_TM = 1024
CAST_ROWS = 512


def _round_up(n, m):
    return (n + m - 1) // m * m


def _pick_tile(n, pref, quantum=V7X_LANES):
    if n <= pref:
        return n
    t = pref - pref % quantum
    while t >= quantum:
        if n % t == 0:
            return t
        t -= quantum
    return n


def _params(n_axes, block_bytes):
    limit = min(VMEM_CAP_BYTES, 2 * block_bytes + 16 * 1024 * 1024)
    return pltpu.CompilerParams(
        dimension_semantics=("arbitrary",) * n_axes, vmem_limit_bytes=int(limit))


def _nbytes(shape, dtype):
    n = 1
    for s in shape:
        n *= s
    return n * jnp.dtype(dtype).itemsize


def _dot(a, b):
    return jnp.dot(a, b, preferred_element_type=F32)


def _dot_nt(a, b):
    return lax.dot_general(a, b, (((1,), (1,)), ((), ())), preferred_element_type=F32)


def _dot_tn(a, b):
    return lax.dot_general(a, b, (((0,), (0,)), ((), ())), preferred_element_type=F32)


def _bf(x):
    return x.astype(BF16)


def _sigmoid(x):
    return 1.0 / (1.0 + jnp.exp(-x))


def _split3(x):
    hi = _bf(x)
    r1 = x - hi.astype(F32)
    mid = _bf(r1)
    lo = _bf(r1 - mid.astype(F32))
    return hi, mid, lo


def _chunk_cumsum(x, tril_bf):
    hi, mid, lo = _split3(x)
    return _dot(tril_bf, hi) + _dot(tril_bf, mid) + _dot(tril_bf, lo)


def _shift_rows(u, prev8, k):
    r, n = u.shape
    g = r // V7X_SUBLANES
    rolled = pltpu.roll(u.reshape(g, V7X_SUBLANES, n), k, 1)
    above = jnp.concatenate([pltpu.roll(prev8, k, 0)[None], rolled[:g - 1]], axis=0)
    sub = lax.broadcasted_iota(jnp.int32, (g, V7X_SUBLANES, n), 1)
    return jnp.where(sub < k, above, rolled).reshape(r, n)


def _cast_rows(src_ref, dst_ref):
    k = src_ref.shape[0]
    step = _pick_tile(k, CAST_ROWS, V7X_SUBLANES)
    for r in range(0, k, step):
        dst_ref[r:r + step, :] = _bf(src_ref[r:r + step, :])


def _rms_kernel(x_ref, g_ref, h_ref):
    x = x_ref[...]
    ms = jnp.mean(x * x, axis=-1, keepdims=True)
    h_ref[...] = (x * lax.rsqrt(ms + NORM_EPS) * g_ref[...]).astype(h_ref.dtype)


def _rms_norm_bf16(x, g):
    t, d = x.shape
    tm = _pick_tile(t, 256, V7X_SUBLANES)
    blk = _nbytes((tm, d), F32) + _nbytes((tm, d), BF16)
    return pl.pallas_call(
        _rms_kernel,
        out_shape=jax.ShapeDtypeStruct((t, d), BF16),
        grid=(t // tm,),
        in_specs=[pl.BlockSpec((tm, d), lambda i: (i, 0)),
                  pl.BlockSpec((1, d), lambda i: (0, 0))],
        out_specs=pl.BlockSpec((tm, d), lambda i: (i, 0)),
        compiler_params=_params(1, blk),
        name="rms_norm",
    )(x, g.reshape(1, d))


def _residual_kernel(o_ref, x_ref, gpost_ref, *rest, with_next):
    o = o_ref[...]
    ms = jnp.mean(o * o, axis=-1, keepdims=True)
    xn = x_ref[...] + o * lax.rsqrt(ms + NORM_EPS) * gpost_ref[...]
    if with_next:
        gpre_ref, xo_ref, h_ref = rest
        ms2 = jnp.mean(xn * xn, axis=-1, keepdims=True)
        h_ref[...] = (xn * lax.rsqrt(ms2 + NORM_EPS) * gpre_ref[...]).astype(h_ref.dtype)
    else:
        (xo_ref,) = rest
    xo_ref[...] = xn


def _residual_norm(o, x, g_post, g_next):
    t, d = x.shape
    tm = _pick_tile(t, 256, V7X_SUBLANES)
    with_next = g_next is not None
    row = pl.BlockSpec((tm, d), lambda i: (i, 0))
    vec = pl.BlockSpec((1, d), lambda i: (0, 0))
    in_specs = [row, row, vec] + ([vec] if with_next else [])
    args = [o, x, g_post.reshape(1, d)] + ([g_next.reshape(1, d)] if with_next else [])
    out_shape = [jax.ShapeDtypeStruct((t, d), F32)]
    out_specs = [row]
    if with_next:
        out_shape.append(jax.ShapeDtypeStruct((t, d), BF16))
        out_specs.append(row)
    blk = 3 * _nbytes((tm, d), F32) + _nbytes((tm, d), BF16)
    res = pl.pallas_call(
        functools.partial(_residual_kernel, with_next=with_next),
        out_shape=out_shape,
        grid=(t // tm,),
        in_specs=in_specs,
        out_specs=out_specs,
        compiler_params=_params(1, blk),
        name="residual_norm",
    )(*args)
    return (res[0], res[1]) if with_next else (res[0], None)


def _mm_kernel(x_ref, w_ref, o_ref):
    o_ref[...] = _dot(x_ref[...], w_ref[...]).astype(o_ref.dtype)


def _matmul(x, w, *, out_dtype, tm, tn, name):
    m, k = x.shape
    _, n = w.shape
    tm = _pick_tile(m, tm, V7X_SUBLANES)
    tn = _pick_tile(n, tn)
    blk = _nbytes((tm, k), x.dtype) + _nbytes((k, tn), w.dtype) + _nbytes((tm, tn), out_dtype) \
        + _nbytes((tm, tn), F32)
    return pl.pallas_call(
        _mm_kernel,
        out_shape=jax.ShapeDtypeStruct((m, n), out_dtype),
        grid=(m // tm, n // tn),
        in_specs=[pl.BlockSpec((tm, k), lambda i, j: (i, 0)), pl.BlockSpec((k, tn), lambda i, j: (0, j))],
        out_specs=pl.BlockSpec((tm, tn), lambda i, j: (i, j)),
        compiler_params=_params(2, blk),
        name=name,
    )(x, w)


def _proj_kernel(x_ref, w_ref, o_ref, wcur_ref, *maybe_prev, off, act):
    j = pl.program_id(0)
    first_row_tile = pl.program_id(1) == 0
    tn = wcur_ref.shape[1]
    if off == 0:
        @pl.when(first_row_tile)
        def _():
            _cast_rows(w_ref, wcur_ref)
        ready = None
    else:
        (wprev_ref,) = maybe_prev
        k = w_ref.shape[0]
        step = _pick_tile(k, CAST_ROWS, V7X_SUBLANES)

        @pl.when(first_row_tile & (j > 0))
        def _():
            for r in range(0, k, step):
                both = jnp.concatenate([wprev_ref[r:r + step, :], w_ref[r:r + step, :]], axis=1)
                wcur_ref[r:r + step, :] = _bf(both[:, off:off + tn])

        @pl.when(first_row_tile)
        def _():
            wprev_ref[...] = w_ref[...]
        ready = j > 0

    def compute():
        acc = _dot(x_ref[...], wcur_ref[...])
        if act == "sigmoid":
            acc = _sigmoid(acc)
        o_ref[...] = acc.astype(o_ref.dtype)

    if ready is None:
        compute()
    else:
        pl.when(ready)(compute)

        @pl.when(jnp.logical_not(ready))
        def _():
            o_ref[...] = jnp.zeros_like(o_ref)


def _proj(x, w, layer, start, width, *, out_dtype, act=None, name):
    m, k = x.shape
    n_total = w.shape[-1]
    tm = _pick_tile(m, PROJ_TM, V7X_SUBLANES)
    tn = PROJ_TN if width % PROJ_TN == 0 else _pick_tile(width, PROJ_TN)
    nt = width // tn
    q0, off = divmod(start, tn)
    shifted = off != 0
    assert (q0 + nt - (0 if shifted else 1)) * tn < n_total
    if shifted:
        grid = (nt + 1, m // tm)
        wmap = lambda j, i: (layer, 0, q0 + j)
        omap = lambda j, i: (i, jnp.maximum(j - 1, 0))
        scratch = [pltpu.VMEM((k, tn), BF16), pltpu.VMEM((k, tn), F32)]
    else:
        grid = (nt, m // tm)
        wmap = lambda j, i: (layer, 0, q0 + j)
        omap = lambda j, i: (i, j)
        scratch = [pltpu.VMEM((k, tn), BF16)]
    blk = _nbytes((tm, k), x.dtype) + _nbytes((k, tn), F32) + _nbytes((tm, tn), out_dtype) \
        + _nbytes((tm, tn), F32) + (_nbytes((k, tn), BF16) + (_nbytes((k, tn), F32) if shifted else 0)) // 2
    return pl.pallas_call(
        functools.partial(_proj_kernel, off=off, act=act),
        out_shape=jax.ShapeDtypeStruct((m, width), out_dtype),
        grid=grid,
        in_specs=[pl.BlockSpec((tm, k), lambda j, i: (i, 0)), pl.BlockSpec((None, k, tn), wmap)],
        out_specs=pl.BlockSpec((tm, tn), omap),
        scratch_shapes=scratch,
        compiler_params=_params(2, blk),
        name=name,
    )(x, w)


def _rwkv_kernel(p_ref, mu_ref, w0_ref, ww2_ref, a0_ref, wa2_ref, wg2_ref, kk_ref, ka_ref, rk_ref,
                 lnw_ref, lnb_ref, y_ref, prev_ref, state_ref, *, e, win_w, win_a, win_g):
    c = p_ref.shape[0]
    pair = 2 * RWKV_HEAD_DIM
    n_pairs = e // pair

    @pl.when(pl.program_id(1) == 0)
    def _():
        prev_ref[...] = jnp.zeros_like(prev_ref)
        state_ref[...] = jnp.zeros_like(state_ref)

    p = p_ref[...]
    shifted = _shift_rows(p, prev_ref[...], 1)
    prev_ref[...] = p[c - V7X_SUBLANES:]
    xm = p + (shifted - p) * mu_ref[...]

    r = xm[:, 0:e]
    k = xm[:, e:2 * e]
    v = xm[:, 2 * e:3 * e]
    wd = xm[:, win_w[0]:win_w[1]]
    ad = xm[:, win_a[0]:win_a[1]]
    gd = xm[:, win_g[0]:win_g[1]]

    w_pre = w0_ref[...] + _dot(_bf(jnp.tanh(wd)), ww2_ref[...])
    lw = (-0.6065306597126334) * _sigmoid(w_pre)
    a = _sigmoid(a0_ref[...] + _dot(_bf(ad), wa2_ref[...]))
    g = _dot(_bf(_sigmoid(gd)), wg2_ref[...])

    ri = lax.broadcasted_iota(jnp.int32, (pair, pair), 0)
    ci = lax.broadcasted_iota(jnp.int32, (pair, pair), 1)
    same_head = (ri < RWKV_HEAD_DIM) == (ci < RWKV_HEAD_DIM)
    head_ones = jnp.where(same_head, 1.0, 0.0).astype(BF16)

    def head_sum(x):
        hi = _bf(x)
        lo = _bf(x - hi.astype(F32))
        xs = jnp.concatenate([hi, lo], axis=0)
        s = jnp.concatenate(
            [_dot(xs[:, j * pair:(j + 1) * pair], head_ones) for j in range(n_pairs)], axis=1)
        return s[:c] + s[c:]

    kk = k * kk_ref[...]
    kk = kk / jnp.maximum(jnp.sqrt(head_sum(kk * kk)), 1e-12)
    kmod = k * (1.0 + (a - 1.0) * ka_ref[...])
    bvec = kk * a

    tri_r = lax.broadcasted_iota(jnp.int32, (c, c), 0)
    tri_c = lax.broadcasted_iota(jnp.int32, (c, c), 1)
    tril = jnp.where(tri_c <= tri_r, 1.0, 0.0).astype(BF16)
    gcum = _chunk_cumsum(lw, tril)
    glast = gcum[c - 1:c, :]
    e_pos = jnp.exp(gcum)
    e_neg = jnp.exp(-gcum)
    e_tail = jnp.exp(glast - gcum)
    r_t = r * e_pos
    kap_t = kk * jnp.exp(gcum - lw)
    b_t = bvec * e_neg
    k_t = kmod * e_neg
    b_dec = bvec * e_tail
    k_dec = kmod * e_tail
    gamma_last = jnp.exp(glast)

    wt = lax.broadcasted_iota(jnp.int32, (c, pair), 0)
    wl = lax.broadcasted_iota(jnp.int32, (c, pair), 1)
    ws = wl & (RWKV_HEAD_DIM - 1)
    lane_lo = wl < RWKV_HEAD_DIM
    m_strict = ws < wt
    m_incl = ws <= wt
    m_16 = m_strict & ((ws >> 4) == (wt >> 4))
    m_32 = m_strict & ((ws >> 5) == (wt >> 5)) & ((ws >> 4) != (wt >> 4))
    m_64 = m_strict & ((ws >> 5) != (wt >> 5))
    eye_w = jnp.where(ws == wt, 1.0, 0.0).astype(F32)
    zero_b = jnp.zeros((c, pair), BF16)

    def stack2(x):
        xb = _bf(x)
        return jnp.concatenate([jnp.where(lane_lo, xb, zero_b), jnp.where(lane_lo, zero_b, xb)], axis=0)

    pairs = range(n_pairs)
    cols = [slice(j * pair, (j + 1) * pair) for j in pairs]
    lhs = [_bf(jnp.concatenate([kap_t[:, s], r_t[:, s]], axis=0)) for s in cols]
    bk_s = [jnp.concatenate([stack2(b_t[:, s]), stack2(k_t[:, s])], axis=0) for s in cols]
    v_s = [stack2(v[:, s]) for s in cols]
    scores = [_dot_nt(lhs[j], bk_s[j]) for j in pairs]
    st = [state_ref[j] for j in pairs]
    proj = [_dot_nt(lhs[j], _bf(st[j])) for j in pairs]
    a_w = [jnp.where(m_strict, scores[j][:c, :pair], 0.0) for j in pairs]
    b_w = [_bf(jnp.where(m_strict, scores[j][:c, pair:], 0.0)) for j in pairs]
    r_w = [_bf(jnp.concatenate([jnp.where(m_incl, scores[j][c:, :pair], 0.0),
                                jnp.where(m_incl, scores[j][c:, pair:], 0.0)], axis=1)) for j in pairs]
    rhs_u = [-(proj[j][:c] + _dot(b_w[j], v_s[j])) for j in pairs]

    d1 = [jnp.where(m_16, a_w[j], 0.0) for j in pairs]
    e32 = [stack2(jnp.where(m_32, a_w[j], 0.0)) for j in pairs]
    e64 = [_bf(jnp.where(m_64, a_w[j], 0.0)) for j in pairs]
    t_inv = [eye_w - d1[j] for j in pairs]
    d2 = [_dot(_bf(d1[j]), stack2(d1[j])) for j in pairs]
    nxt = [_dot(_bf(jnp.concatenate([t_inv[j], d2[j]], axis=0)), stack2(d2[j])) for j in pairs]
    t_inv = [t_inv[j] + nxt[j][:c] for j in pairs]
    d4 = [nxt[j][c:] for j in pairs]
    nxt = [_dot(_bf(jnp.concatenate([t_inv[j], d4[j]], axis=0)), stack2(d4[j])) for j in pairs]
    t_inv = [t_inv[j] + nxt[j][:c] for j in pairs]
    t_inv = [t_inv[j] + _dot(_bf(t_inv[j]), stack2(nxt[j][c:])) for j in pairs]
    te = [_dot(_bf(t_inv[j]), e32[j]) for j in pairs]
    t32 = [_bf(t_inv[j] - _dot(_bf(te[j]), stack2(t_inv[j]))) for j in pairs]
    z = [_dot(t32[j], stack2(rhs_u[j])) for j in pairs]
    ez = [_dot(e64[j], stack2(z[j])) for j in pairs]
    u = [z[j] - _dot(t32[j], stack2(ez[j])) for j in pairs]

    uv_s = [jnp.concatenate([stack2(u[j]), v_s[j]], axis=0) for j in pairs]
    y_cols = [proj[j][c:] + _dot(r_w[j], uv_s[j]) for j in pairs]

    rj = lax.broadcasted_iota(jnp.int32, (pair, pair), 0)
    cj = lax.broadcasted_iota(jnp.int32, (pair, pair), 1)
    same_head_sq = (rj < RWKV_HEAD_DIM) == (cj < RWKV_HEAD_DIM)
    for j in pairs:
        s = cols[j]
        uv = _bf(jnp.concatenate([u[j], v[:, s]], axis=0))
        bk = _bf(jnp.concatenate([b_dec[:, s], k_dec[:, s]], axis=0))
        ds = jnp.where(same_head_sq, _dot_tn(uv, bk), 0.0)
        state_ref[j] = st[j] * gamma_last[:, s] + ds

    y = jnp.concatenate(y_cols, axis=1)
    inv_n = 1.0 / RWKV_HEAD_DIM
    mean = head_sum(y) * inv_n
    yc = y - mean
    var = head_sum(yc * yc) * inv_n
    yn = yc * lax.rsqrt(var + RWKV_GN_EPS) * lnw_ref[...] + lnb_ref[...]
    bonus = head_sum(r * kmod * rk_ref[...]) * v
    y_ref[...] = ((yn + bonus) * g).astype(y_ref.dtype)


def _lane_window(start, length):
    return (start // V7X_LANES * V7X_LANES, _round_up(start + length, V7X_LANES))


def _rows_in_window(w, window, start):
    top = start - window[0]
    return jnp.pad(w, ((top, window[1] - window[0] - top - w.shape[0]), (0, 0))).astype(BF16)


def _rwkv_mix(p, mu, w0, w_w2, a0, w_a2, w_g2, k_k, k_a, r_k, ln_w, ln_b, *, batch, seq, e):
    t, width = p.shape
    lw, la, lg = w_w2.shape[0], w_a2.shape[0], w_g2.shape[0]
    assert CHUNK == RWKV_HEAD_DIM and e % (2 * RWKV_HEAD_DIM) == 0 and seq % CHUNK == 0
    win_w = _lane_window(3 * e, lw)
    win_a = _lane_window(3 * e + lw, la)
    win_g = _lane_window(3 * e + lw + la, lg)
    ww2 = _rows_in_window(w_w2, win_w, 3 * e)
    wa2 = _rows_in_window(w_a2, win_a, 3 * e + lw)
    wg2 = _rows_in_window(w_g2, win_g, 3 * e + lw + la)
    mu_p = jnp.pad(mu, (0, width - mu.shape[0])).reshape(1, width)
    row = lambda a: a.reshape(1, -1)
    nc = seq // CHUNK
    n_pairs = e // (2 * RWKV_HEAD_DIM)
    vec = lambda n: pl.BlockSpec((1, n), lambda b, i: (0, 0))
    full = lambda a: pl.BlockSpec(a.shape, lambda b, i: (0, 0))
    blk = _nbytes((CHUNK, width), F32) + _nbytes((CHUNK, e), BF16) \
        + sum(_nbytes(a.shape, a.dtype) for a in (ww2, wa2, wg2)) + 24 * _nbytes((CHUNK, e), F32)
    return pl.pallas_call(
        functools.partial(_rwkv_kernel, e=e, win_w=win_w, win_a=win_a, win_g=win_g),
        out_shape=jax.ShapeDtypeStruct((t, e), BF16),
        grid=(batch, nc),
        in_specs=[pl.BlockSpec((CHUNK, width), lambda b, i: (b * nc + i, 0)),
                  vec(width), vec(e), full(ww2), vec(e), full(wa2), full(wg2),
                  vec(e), vec(e), vec(e), vec(e), vec(e)],
        out_specs=pl.BlockSpec((CHUNK, e), lambda b, i: (b * nc + i, 0)),
        scratch_shapes=[pltpu.VMEM((V7X_SUBLANES, width), F32),
                        pltpu.VMEM((n_pairs, 2 * RWKV_HEAD_DIM, 2 * RWKV_HEAD_DIM), F32)],
        compiler_params=_params(2, blk),
        name="rwkv7_mix",
    )(p, mu_p, row(w0), ww2, row(a0), wa2, wg2, row(k_k), row(k_a), row(r_k), row(ln_w), row(ln_b))


def _gla_kernel(p_ref, go_ref, wa2_ref, ba_ref, ng_ref, y_ref, state_ref, *, heads, dk, dv):
    c = p_ref.shape[0]
    kw = heads * dk
    e = heads * dv

    @pl.when(pl.program_id(1) == 0)
    def _():
        state_ref[...] = jnp.zeros_like(state_ref)

    p = p_ref[...]
    q = p[:, 0:kw]
    k = p[:, kw:2 * kw]
    v = p[:, 2 * kw:2 * kw + e]
    ad = p[:, 2 * kw + e:2 * kw + e + wa2_ref.shape[0]]
    go = go_ref[...]

    z = _dot(_bf(ad), wa2_ref[...]) + ba_ref[...]
    gk = (jnp.minimum(z, 0.0) - jnp.log(1.0 + jnp.exp(-jnp.abs(z)))) * (1.0 / GLA_GATE_NORM)

    tri_r = lax.broadcasted_iota(jnp.int32, (c, c), 0)
    tri_c = lax.broadcasted_iota(jnp.int32, (c, c), 1)
    causal = tri_c <= tri_r
    tril = jnp.where(causal, 1.0, 0.0).astype(BF16)
    bcum = _chunk_cumsum(gk, tril)
    blast = bcum[c - 1:c, :]
    q_dec = _bf(q * (dk ** -0.5) * jnp.exp(bcum))
    k_inv = _bf(k * jnp.exp(-bcum))
    k_dec = _bf(k * jnp.exp(blast - bcum))
    chunk_decay = jnp.exp(blast)

    outs = []
    for h in range(heads):
        ks = slice(h * dk, (h + 1) * dk)
        vh = _bf(v[:, h * dv:(h + 1) * dv])
        att = jnp.where(causal, _dot_nt(q_dec[:, ks], k_inv[:, ks]), 0.0)
        st = state_ref[h]
        oh = _dot(_bf(att), vh) + _dot_nt(q_dec[:, ks], _bf(st))
        state_ref[h] = st * chunk_decay[:, ks] + _dot_tn(vh, k_dec[:, ks])
        ms = jnp.mean(oh * oh, axis=-1, keepdims=True)
        outs.append(oh * lax.rsqrt(ms + GLA_NORM_EPS) * ng_ref[...])
    o = jnp.concatenate(outs, axis=1)
    y_ref[...] = (o * (go * _sigmoid(go))).astype(y_ref.dtype)


def _gla_mix(p, go, w_a2, b_a, norm_g, *, batch, seq, heads, dk, dv):
    t, width = p.shape
    e = heads * dv
    wa2 = jnp.pad(w_a2, ((0, V7X_LANES - w_a2.shape[0]), (0, 0))).astype(BF16)
    nc = seq // CHUNK
    blk = _nbytes((CHUNK, width), F32) + _nbytes((CHUNK, e), F32) + _nbytes((CHUNK, e), BF16) \
        + _nbytes(wa2.shape, wa2.dtype) + 16 * _nbytes((CHUNK, e), F32)
    return pl.pallas_call(
        functools.partial(_gla_kernel, heads=heads, dk=dk, dv=dv),
        out_shape=jax.ShapeDtypeStruct((t, e), BF16),
        grid=(batch, nc),
        in_specs=[pl.BlockSpec((CHUNK, width), lambda b, i: (b * nc + i, 0)),
                  pl.BlockSpec((CHUNK, e), lambda b, i: (b * nc + i, 0)),
                  pl.BlockSpec(wa2.shape, lambda b, i: (0, 0)),
                  pl.BlockSpec((1, heads * dk), lambda b, i: (0, 0)),
                  pl.BlockSpec((1, dv), lambda b, i: (0, 0))],
        out_specs=pl.BlockSpec((CHUNK, e), lambda b, i: (b * nc + i, 0)),
        scratch_shapes=[pltpu.VMEM((heads, dv, dk), F32)],
        compiler_params=_params(2, blk),
        name="gla_mix",
    )(p, go, wa2, b_a.reshape(1, -1), norm_g.reshape(1, -1))


def _causal_conv3(u, prev8, w):
    return _shift_rows(u, prev8, 2) * w[0:1] + _shift_rows(u, prev8, 1) * w[1:2] + u * w[2:3]


def _sconv_kernel(bg_ref, cg_ref, hh_ref, w_ref, y_ref, carry_ref):
    @pl.when(pl.program_id(1) == 0)
    def _():
        carry_ref[...] = jnp.zeros_like(carry_ref)

    u = cg_ref[...] * hh_ref[...]
    y = bg_ref[...] * _causal_conv3(u, carry_ref[...], w_ref[...])
    carry_ref[...] = u[u.shape[0] - V7X_SUBLANES:]
    y_ref[...] = y.astype(y_ref.dtype)


def _sconv_mix(p, conv_w, *, batch, seq, e):
    t = p.shape[0]
    tm = _pick_tile(seq, 512, V7X_SUBLANES)
    ms = seq // tm
    col = lambda g: pl.BlockSpec((tm, e), lambda b, i, g=g: (b * ms + i, g))
    blk = 3 * _nbytes((tm, e), F32) + _nbytes((tm, e), BF16) + 4 * _nbytes((tm, e), F32)
    return pl.pallas_call(
        _sconv_kernel,
        out_shape=jax.ShapeDtypeStruct((t, e), BF16),
        grid=(batch, ms),
        in_specs=[col(0), col(1), col(2), pl.BlockSpec((CONV_WIDTH, e), lambda b, i: (0, 0))],
        out_specs=pl.BlockSpec((tm, e), lambda b, i: (b * ms + i, 0)),
        scratch_shapes=[pltpu.VMEM((V7X_SUBLANES, e), F32)],
        compiler_params=_params(2, blk),
        name="short_conv_mix",
    )(p, p, p, conv_w)


def _merge_kernel(ya_ref, yb_ref, yc_ref, wa_ref, wb_ref, wc_ref, ga_ref, gb_ref, gc_ref, o_ref, wbf_ref):
    @pl.when(pl.program_id(1) == 0)
    def _():
        for g, w_ref in enumerate((wa_ref, wb_ref, wc_ref)):
            _cast_rows(w_ref, wbf_ref.at[g])

    acc = ga_ref[...].astype(F32) * _dot(ya_ref[...], wbf_ref[0])
    acc = acc + gb_ref[...].astype(F32) * _dot(yb_ref[...], wbf_ref[1])
    acc = acc + gc_ref[...].astype(F32) * _dot(yc_ref[...], wbf_ref[2])
    o_ref[...] = acc.astype(o_ref.dtype)


def _merge(ya, yb, yc, w_br, layer, gates):
    t, e = ya.shape
    d = w_br.shape[-1]
    tm = _pick_tile(t, 512, V7X_SUBLANES)
    tn = _pick_tile(d, 512)
    nt = d // tn
    ysp = pl.BlockSpec((tm, e), lambda j, i: (i, 0))
    wsp = lambda g: pl.BlockSpec((None, None, e, tn), lambda j, i, g=g: (layer, g, 0, j))
    gsp = lambda g: pl.BlockSpec((tm, tn), lambda j, i, g=g: (i, g * nt + j))
    blk = 3 * _nbytes((tm, e), BF16) + 3 * _nbytes((e, tn), F32) + 3 * _nbytes((tm, tn), gates.dtype) \
        + _nbytes((tm, tn), BF16) + 2 * _nbytes((tm, tn), F32) + 3 * _nbytes((e, tn), BF16) // 2
    return pl.pallas_call(
        _merge_kernel,
        out_shape=jax.ShapeDtypeStruct((t, d), BF16),
        grid=(nt, t // tm),
        in_specs=[ysp, ysp, ysp, wsp(0), wsp(1), wsp(2), gsp(0), gsp(1), gsp(2)],
        out_specs=pl.BlockSpec((tm, tn), lambda j, i: (i, j)),
        scratch_shapes=[pltpu.VMEM((3, e, tn), BF16)],
        compiler_params=_params(2, blk),
        name="gated_merge",
    )(ya, yb, yc, w_br, w_br, w_br, gates, gates, gates)


def _ffn_up_kernel(h_ref, wg_ref, wv_ref, cg_ref, cv_ref, act_ref, wbf_ref, carry_ref):
    b = pl.program_id(1)
    m = pl.program_id(2)

    @pl.when((b == 0) & (m == 0))
    def _():
        _cast_rows(wg_ref, wbf_ref.at[0])
        _cast_rows(wv_ref, wbf_ref.at[1])

    @pl.when(m == 0)
    def _():
        carry_ref[...] = jnp.zeros_like(carry_ref)

    h = h_ref[...]
    ug = _dot(h, wbf_ref[0])
    uv = _dot(h, wbf_ref[1])
    tail = ug.shape[0] - V7X_SUBLANES
    prev_g = carry_ref[0]
    prev_v = carry_ref[1]
    carry_ref[0] = ug[tail:]
    carry_ref[1] = uv[tail:]
    xg = _causal_conv3(ug, prev_g, cg_ref[...])
    xv = _causal_conv3(uv, prev_v, cv_ref[...])
    gelu = 0.5 * xg * (1.0 + jnp.tanh(0.7978845608028654 * (xg + 0.044715 * (xg * xg * xg))))
    act_ref[...] = (gelu * xv).astype(act_ref.dtype)


def _ffn_up(h, w_up, layer, conv_w, *, batch, seq):
    t, d = h.shape
    f = w_up.shape[-1] // 2
    tm = _pick_tile(seq, 1024, V7X_SUBLANES)
    tn = _pick_tile(f, 256)
    ms, nt = seq // tm, f // tn
    blk = _nbytes((tm, d), BF16) + 2 * _nbytes((d, tn), F32) + _nbytes((tm, tn), BF16) \
        + 8 * _nbytes((tm, tn), F32) + _nbytes((d, tn), BF16)
    return pl.pallas_call(
        _ffn_up_kernel,
        out_shape=jax.ShapeDtypeStruct((t, f), BF16),
        grid=(nt, batch, ms),
        in_specs=[pl.BlockSpec((tm, d), lambda j, b, i: (b * ms + i, 0)),
                  pl.BlockSpec((None, d, tn), lambda j, b, i: (layer, 0, j)),
                  pl.BlockSpec((None, d, tn), lambda j, b, i: (layer, 0, nt + j)),
                  pl.BlockSpec((CONV_WIDTH, tn), lambda j, b, i: (0, j)),
                  pl.BlockSpec((CONV_WIDTH, tn), lambda j, b, i: (0, nt + j))],
        out_specs=pl.BlockSpec((tm, tn), lambda j, b, i: (b * ms + i, j)),
        scratch_shapes=[pltpu.VMEM((2, d, tn), BF16),
                        pltpu.VMEM((2, V7X_SUBLANES, tn), F32)],
        compiler_params=_params(3, blk),
        name="ffn_up_conv_gate",
    )(h, w_up, w_up, conv_w, conv_w)


def _layer(x, h, l, prm, next_pre_g, *, batch, seq):
    (w_in, rwkv_mu, rwkv_w0, rwkv_w_w2, rwkv_a0, rwkv_w_a2, rwkv_w_g2, rwkv_k_k, rwkv_k_a, rwkv_r_k,
     rwkv_ln_w, rwkv_ln_b, gla_w_a2, gla_b_a, gla_norm_g, sconv_w, w_br, w_o, post_mix_g, pre_ffn_g,
     w_up, ffn_conv_w, w_down, post_ffn_g) = prm
    d = x.shape[1]
    e = w_br.shape[2]
    lora = rwkv_w_w2.shape[1] + rwkv_w_a2.shape[1] + rwkv_w_g2.shape[1]
    gr, kw = gla_w_a2.shape[1:]
    dv = gla_norm_g.shape[1]
    heads = e // dv
    dk = kw // heads

    o_gla = 3 * e + lora
    o_go = o_gla + 2 * kw + e + gr
    o_conv = o_go + e
    o_gate = o_conv + 3 * e
    p_rwkv = _proj(h, w_in, l, 0, _round_up(o_gla, PROJ_TN), out_dtype=F32, name="proj_rwkv")
    p_gla = _proj(h, w_in, l, o_gla, _round_up(2 * kw + e + V7X_LANES, PROJ_TN), out_dtype=F32,
                  name="proj_gla")
    p_go = _proj(h, w_in, l, o_go, e, out_dtype=F32, name="proj_gla_gate")
    p_conv = _proj(h, w_in, l, o_conv, 3 * e, out_dtype=F32, name="proj_conv")
    gates = _proj(h, w_in, l, o_gate, 3 * d, out_dtype=BF16, act="sigmoid", name="proj_gate")

    y_a = _rwkv_mix(p_rwkv, rwkv_mu[l], rwkv_w0[l], rwkv_w_w2[l], rwkv_a0[l], rwkv_w_a2[l], rwkv_w_g2[l],
                    rwkv_k_k[l], rwkv_k_a[l], rwkv_r_k[l], rwkv_ln_w[l], rwkv_ln_b[l],
                    batch=batch, seq=seq, e=e)
    y_b = _gla_mix(p_gla, p_go, gla_w_a2[l], gla_b_a[l], gla_norm_g[l],
                   batch=batch, seq=seq, heads=heads, dk=dk, dv=dv)
    y_c = _sconv_mix(p_conv, sconv_w[l], batch=batch, seq=seq, e=e)

    merged = _merge(y_a, y_b, y_c, w_br, l, gates)
    o = _proj(merged, w_o, l, 0, d, out_dtype=F32, name="proj_out")
    x, h2 = _residual_norm(o, x, post_mix_g[l], pre_ffn_g[l])

    act = _ffn_up(h2, w_up, l, ffn_conv_w[l], batch=batch, seq=seq)
    o2 = _matmul(act, w_down[l].astype(BF16), out_dtype=F32, tm=512, tn=512, name="ffn_down")
    return _residual_norm(o2, x, post_ffn_g[l], next_pre_g)


def kernel(x, pre_mix_g, w_in, rwkv_mu, rwkv_w0, rwkv_w_w2, rwkv_a0, rwkv_w_a2, rwkv_w_g2, rwkv_k_k, rwkv_k_a, rwkv_r_k, rwkv_ln_w, rwkv_ln_b, gla_w_a2, gla_b_a, gla_norm_g, sconv_w, w_br, w_o, post_mix_g, pre_ffn_g, w_up, ffn_conv_w, w_down, post_ffn_g):
    batch, seq, d = x.shape
    depth = w_in.shape[0]
    prm = (w_in, rwkv_mu, rwkv_w0, rwkv_w_w2, rwkv_a0, rwkv_w_a2, rwkv_w_g2, rwkv_k_k, rwkv_k_a,
           rwkv_r_k.reshape(depth, -1), rwkv_ln_w, rwkv_ln_b, gla_w_a2, gla_b_a, gla_norm_g, sconv_w,
           w_br, w_o, post_mix_g, pre_ffn_g, w_up, ffn_conv_w, w_down, post_ffn_g)
    xf = x.reshape(batch * seq, d)
    h = _rms_norm_bf16(xf, pre_mix_g[0])
    for l in range(depth):
        next_g = pre_mix_g[l + 1] if l + 1 < depth else None
        xf, h = _layer(xf, h, l, prm, next_g, batch=batch, seq=seq)
    return xf.reshape(batch, seq, d)
```

```python
import functools

import jax
import jax.numpy as jnp
from jax import lax
from jax.experimental import pallas as pl
from jax.experimental.pallas import tpu as pltpu

F32 = jnp.float32
BF16 = jnp.bfloat16

NORM_EPS = 1e-6
RWKV_HEAD_DIM = 64
RWKV_GN_EPS = 1e-5 * RWKV_HEAD_DIM
GLA_GATE_NORM = 16.0
GLA_NORM_EPS = 1e-5
CONV_WIDTH = 3
CHUNK = 64

V7X_LANES = 128
V7X_SUBLANES = 8
V7X_VMEM_BYTES = 64 * 1024 * 1024
VMEM_CAP_BYTES = V7X_VMEM_BYTES - 6 * 1024 * 1024

CAST_VREGS = 256


def _round_up(n, m):
    return (n + m - 1) // m * m


def _pick_tile(n, pref, quantum=V7X_LANES):
    if n <= pref:
        return n
    t = pref - pref % quantum
    while t >= quantum:
        if n % t == 0:
            return t
        t -= quantum
    return n


def _params(n_axes, block_bytes):
    limit = min(VMEM_CAP_BYTES, 2 * block_bytes + 16 * 1024 * 1024)
    return pltpu.CompilerParams(
        dimension_semantics=("arbitrary",) * n_axes, vmem_limit_bytes=int(limit))


def _nbytes(shape, dtype):
    n = 1
    for s in shape:
        n *= s
    return n * jnp.dtype(dtype).itemsize


def _dot(a, b):
    return jnp.dot(a, b, preferred_element_type=F32)


def _dot_nt(a, b):
    return lax.dot_general(a, b, (((1,), (1,)), ((), ())), preferred_element_type=F32)


def _dot_tn(a, b):
    return lax.dot_general(a, b, (((0,), (0,)), ((), ())), preferred_element_type=F32)


def _bf(x):
    return x.astype(BF16)


def _sigmoid(x):
    return 1.0 / (1.0 + jnp.exp(-x))


def _split3(x):
    hi = _bf(x)
    r1 = x - hi.astype(F32)
    mid = _bf(r1)
    lo = _bf(r1 - mid.astype(F32))
    return hi, mid, lo


def _chunk_cumsum(x, tril_bf):
    hi, mid, lo = _split3(x)
    return _dot(tril_bf, hi) + _dot(tril_bf, mid) + _dot(tril_bf, lo)


def _shift_rows(u, prev8, k):
    r, n = u.shape
    g = r // V7X_SUBLANES
    rolled = pltpu.roll(u.reshape(g, V7X_SUBLANES, n), k, 1)
    above = jnp.concatenate([pltpu.roll(prev8, k, 0)[None], rolled[:g - 1]], axis=0)
    sub = lax.broadcasted_iota(jnp.int32, (g, V7X_SUBLANES, n), 1)
    return jnp.where(sub < k, above, rolled).reshape(r, n)


def _cast_rows(src_ref, dst_ref):
    rows, width = src_ref.shape
    step = _pick_tile(rows, max(CAST_VREGS * V7X_SUBLANES * V7X_LANES // width, 2 * V7X_SUBLANES),
                      2 * V7X_SUBLANES)
    for r in range(0, rows, step):
        dst_ref[r:r + step, :] = _bf(src_ref[r:r + step, :])


def _rms_kernel(x_ref, g_ref, h_ref):
    x = x_ref[...]
    ms = jnp.mean(x * x, axis=-1, keepdims=True)
    h_ref[...] = (x * lax.rsqrt(ms + NORM_EPS) * g_ref[...]).astype(h_ref.dtype)


def _rms_norm_bf16(x, g):
    t, d = x.shape
    tm = _pick_tile(t, 256, V7X_SUBLANES)
    blk = _nbytes((tm, d), F32) + _nbytes((tm, d), BF16)
    return pl.pallas_call(
        _rms_kernel,
        out_shape=jax.ShapeDtypeStruct((t, d), BF16),
        grid=(t // tm,),
        in_specs=[pl.BlockSpec((tm, d), lambda i: (i, 0)),
                  pl.BlockSpec((1, d), lambda i: (0, 0))],
        out_specs=pl.BlockSpec((tm, d), lambda i: (i, 0)),
        compiler_params=_params(1, blk),
        name="rms_norm",
    )(x, g.reshape(1, d))


def _residual_kernel(o_ref, x_ref, gpost_ref, *rest, with_next):
    o = o_ref[...]
    ms = jnp.mean(o * o, axis=-1, keepdims=True)
    xn = x_ref[...] + o * lax.rsqrt(ms + NORM_EPS) * gpost_ref[...]
    if with_next:
        gpre_ref, xo_ref, h_ref = rest
        ms2 = jnp.mean(xn * xn, axis=-1, keepdims=True)
        h_ref[...] = (xn * lax.rsqrt(ms2 + NORM_EPS) * gpre_ref[...]).astype(h_ref.dtype)
    else:
        (xo_ref,) = rest
    xo_ref[...] = xn


def _residual_norm(o, x, g_post, g_next):
    t, d = x.shape
    tm = _pick_tile(t, 256, V7X_SUBLANES)
    with_next = g_next is not None
    row = pl.BlockSpec((tm, d), lambda i: (i, 0))
    vec = pl.BlockSpec((1, d), lambda i: (0, 0))
    in_specs = [row, row, vec] + ([vec] if with_next else [])
    args = [o, x, g_post.reshape(1, d)] + ([g_next.reshape(1, d)] if with_next else [])
    out_shape = [jax.ShapeDtypeStruct((t, d), F32)]
    out_specs = [row]
    if with_next:
        out_shape.append(jax.ShapeDtypeStruct((t, d), BF16))
        out_specs.append(row)
    blk = 3 * _nbytes((tm, d), F32) + _nbytes((tm, d), BF16)
    res = pl.pallas_call(
        functools.partial(_residual_kernel, with_next=with_next),
        out_shape=out_shape,
        grid=(t // tm,),
        in_specs=in_specs,
        out_specs=out_specs,
        compiler_params=_params(1, blk),
        name="residual_norm",
    )(*args)
    return (res[0], res[1]) if with_next else (res[0], None)


def _mm_kernel(x_ref, w_ref, o_ref):
    o_ref[...] = _dot(x_ref[...], w_ref[...]).astype(o_ref.dtype)


def _matmul(x, w, *, out_dtype, tm, tn, name):
    m, k = x.shape
    _, n = w.shape
    tm = _pick_tile(m, tm, V7X_SUBLANES)
    tn = _pick_tile(n, tn)
    blk = _nbytes((tm, k), x.dtype) + _nbytes((k, tn), w.dtype) + _nbytes((tm, tn), out_dtype) \
        + _nbytes((tm, tn), F32)
    return pl.pallas_call(
        _mm_kernel,
        out_shape=jax.ShapeDtypeStruct((m, n), out_dtype),
        grid=(m // tm, n // tn),
        in_specs=[pl.BlockSpec((tm, k), lambda i, j: (i, 0)), pl.BlockSpec((k, tn), lambda i, j: (0, j))],
        out_specs=pl.BlockSpec((tm, tn), lambda i, j: (i, j)),
        compiler_params=_params(2, blk),
        name=name,
    )(x, w)


PROJ_TN_CHOICES = (1024, 768, 512)


def _tile_columns(need):
    return min(((_round_up(need, tn), tn) for tn in PROJ_TN_CHOICES), key=lambda wt: (wt[0], -wt[1]))


def _proj_kernel(x_ref, w_ref, o_ref, wbf_ref, *, w_is_nk, act):
    @pl.when(pl.program_id(1) == 0)
    def _():
        _cast_rows(w_ref, wbf_ref)

    acc = _dot_nt(x_ref[...], wbf_ref[...]) if w_is_nk else _dot(x_ref[...], wbf_ref[...])
    if act == "sigmoid":
        acc = _sigmoid(acc)
    o_ref[...] = acc.astype(o_ref.dtype)


def _proj(x, w, layer, start, width, *, w_is_nk, tn, out_dtype, act=None, name):
    m, k = x.shape
    assert width % tn == 0
    out_bytes = _nbytes((1, tn), out_dtype)
    w_bytes = _nbytes((k, tn), F32)

    def fits(t):
        need = 2 * (_nbytes((t, k), x.dtype) + w_bytes + t * out_bytes) + w_bytes // 2 + _nbytes((t, tn), F32)
        return need <= VMEM_CAP_BYTES - 4 * 1024 * 1024

    tm = next((t for t in (1024, 512, 256, 128) if m % t == 0 and fits(t)), m)
    if w_is_nk:
        assert start % V7X_SUBLANES == 0 and start + width <= w.shape[1]
        wspec = pl.BlockSpec((pl.Squeezed(), pl.Element(tn), pl.Element(k)),
                             lambda j, i: (layer, pl.multiple_of(start + j * tn, V7X_SUBLANES), 0))
        wshape = (tn, k)
    else:
        assert start % tn == 0 and start + width <= w.shape[2]
        wspec = pl.BlockSpec((None, k, tn), lambda j, i: (layer, 0, start // tn + j))
        wshape = (k, tn)
    blk = _nbytes((tm, k), x.dtype) + w_bytes + tm * out_bytes + (_nbytes((tm, tn), F32) + w_bytes // 2) // 2
    return pl.pallas_call(
        functools.partial(_proj_kernel, w_is_nk=w_is_nk, act=act),
        out_shape=jax.ShapeDtypeStruct((m, width), out_dtype),
        grid=(width // tn, m // tm),
        in_specs=[pl.BlockSpec((tm, k), lambda j, i: (i, 0)), wspec],
        out_specs=pl.BlockSpec((tm, tn), lambda j, i: (i, j)),
        scratch_shapes=[pltpu.VMEM(wshape, BF16)],
        compiler_params=_params(2, blk),
        name=name,
    )(x, w)


def _rwkv_kernel(p_ref, mu_ref, w0_ref, ww2_ref, a0_ref, wa2_ref, wg2_ref, kk_ref, ka_ref, rk_ref,
                 lnw_ref, lnb_ref, y_ref, prev_ref, state_ref, *, e, win_w, win_a, win_g):
    c = p_ref.shape[0]
    pair = 2 * RWKV_HEAD_DIM
    n_pairs = e // pair

    @pl.when(pl.program_id(1) == 0)
    def _():
        prev_ref[...] = jnp.zeros_like(prev_ref)
        state_ref[...] = jnp.zeros_like(state_ref)

    p = p_ref[...]
    shifted = _shift_rows(p, prev_ref[...], 1)
    prev_ref[...] = p[c - V7X_SUBLANES:]
    xm = p + (shifted - p) * mu_ref[...]

    r = xm[:, 0:e]
    k = xm[:, e:2 * e]
    v = xm[:, 2 * e:3 * e]
    wd = xm[:, win_w[0]:win_w[1]]
    ad = xm[:, win_a[0]:win_a[1]]
    gd = xm[:, win_g[0]:win_g[1]]

    w_pre = w0_ref[...] + _dot(_bf(jnp.tanh(wd)), ww2_ref[...])
    lw = (-0.6065306597126334) * _sigmoid(w_pre)
    a = _sigmoid(a0_ref[...] + _dot(_bf(ad), wa2_ref[...]))
    g = _dot(_bf(_sigmoid(gd)), wg2_ref[...])

    ri = lax.broadcasted_iota(jnp.int32, (pair, pair), 0)
    ci = lax.broadcasted_iota(jnp.int32, (pair, pair), 1)
    same_head = (ri < RWKV_HEAD_DIM) == (ci < RWKV_HEAD_DIM)
    head_ones = jnp.where(same_head, 1.0, 0.0).astype(BF16)

    def head_sum(x):
        hi = _bf(x)
        lo = _bf(x - hi.astype(F32))
        xs = jnp.concatenate([hi, lo], axis=0)
        s = jnp.concatenate(
            [_dot(xs[:, j * pair:(j + 1) * pair], head_ones) for j in range(n_pairs)], axis=1)
        return s[:c] + s[c:]

    kk = k * kk_ref[...]
    kk = kk / jnp.maximum(jnp.sqrt(head_sum(kk * kk)), 1e-12)
    kmod = k * (1.0 + (a - 1.0) * ka_ref[...])
    bvec = kk * a

    tri_r = lax.broadcasted_iota(jnp.int32, (c, c), 0)
    tri_c = lax.broadcasted_iota(jnp.int32, (c, c), 1)
    tril = jnp.where(tri_c <= tri_r, 1.0, 0.0).astype(BF16)
    gcum = _chunk_cumsum(lw, tril)
    glast = gcum[c - 1:c, :]
    e_pos = jnp.exp(gcum)
    e_neg = jnp.exp(-gcum)
    e_tail = jnp.exp(glast - gcum)
    r_t = r * e_pos
    kap_t = kk * jnp.exp(gcum - lw)
    b_t = bvec * e_neg
    k_t = kmod * e_neg
    b_dec = bvec * e_tail
    k_dec = kmod * e_tail
    gamma_last = jnp.exp(glast)

    wt = lax.broadcasted_iota(jnp.int32, (c, pair), 0)
    wl = lax.broadcasted_iota(jnp.int32, (c, pair), 1)
    ws = wl & (RWKV_HEAD_DIM - 1)
    lane_lo = wl < RWKV_HEAD_DIM
    m_strict = ws < wt
    m_incl = ws <= wt
    m_16 = m_strict & ((ws >> 4) == (wt >> 4))
    m_32 = m_strict & ((ws >> 5) == (wt >> 5)) & ((ws >> 4) != (wt >> 4))
    m_64 = m_strict & ((ws >> 5) != (wt >> 5))
    eye_w = jnp.where(ws == wt, 1.0, 0.0).astype(F32)
    zero_b = jnp.zeros((c, pair), BF16)

    def stack2(x):
        xb = _bf(x)
        return jnp.concatenate([jnp.where(lane_lo, xb, zero_b), jnp.where(lane_lo, zero_b, xb)], axis=0)

    pairs = range(n_pairs)
    cols = [slice(j * pair, (j + 1) * pair) for j in pairs]
    lhs = [_bf(jnp.concatenate([kap_t[:, s], r_t[:, s]], axis=0)) for s in cols]
    bk_s = [jnp.concatenate([stack2(b_t[:, s]), stack2(k_t[:, s])], axis=0) for s in cols]
    v_s = [stack2(v[:, s]) for s in cols]
    scores = [_dot_nt(lhs[j], bk_s[j]) for j in pairs]
    st = [state_ref[j] for j in pairs]
    proj = [_dot_nt(lhs[j], _bf(st[j])) for j in pairs]
    a_w = [jnp.where(m_strict, scores[j][:c, :pair], 0.0) for j in pairs]
    b_w = [_bf(jnp.where(m_strict, scores[j][:c, pair:], 0.0)) for j in pairs]
    r_w = [_bf(jnp.concatenate([jnp.where(m_incl, scores[j][c:, :pair], 0.0),
                                jnp.where(m_incl, scores[j][c:, pair:], 0.0)], axis=1)) for j in pairs]
    rhs_u = [-(proj[j][:c] + _dot(b_w[j], v_s[j])) for j in pairs]

    d1 = [jnp.where(m_16, a_w[j], 0.0) for j in pairs]
    e32 = [stack2(jnp.where(m_32, a_w[j], 0.0)) for j in pairs]
    e64 = [_bf(jnp.where(m_64, a_w[j], 0.0)) for j in pairs]
    t_inv = [eye_w - d1[j] for j in pairs]
    d2 = [_dot(_bf(d1[j]), stack2(d1[j])) for j in pairs]
    nxt = [_dot(_bf(jnp.concatenate([t_inv[j], d2[j]], axis=0)), stack2(d2[j])) for j in pairs]
    t_inv = [t_inv[j] + nxt[j][:c] for j in pairs]
    d4 = [nxt[j][c:] for j in pairs]
    nxt = [_dot(_bf(jnp.concatenate([t_inv[j], d4[j]], axis=0)), stack2(d4[j])) for j in pairs]
    t_inv = [t_inv[j] + nxt[j][:c] for j in pairs]
    t_inv = [t_inv[j] + _dot(_bf(t_inv[j]), stack2(nxt[j][c:])) for j in pairs]
    te = [_dot(_bf(t_inv[j]), e32[j]) for j in pairs]
    t32 = [_bf(t_inv[j] - _dot(_bf(te[j]), stack2(t_inv[j]))) for j in pairs]
    z = [_dot(t32[j], stack2(rhs_u[j])) for j in pairs]
    ez = [_dot(e64[j], stack2(z[j])) for j in pairs]
    u = [z[j] - _dot(t32[j], stack2(ez[j])) for j in pairs]

    uv_s = [jnp.concatenate([stack2(u[j]), v_s[j]], axis=0) for j in pairs]
    y_cols = [proj[j][c:] + _dot(r_w[j], uv_s[j]) for j in pairs]

    rj = lax.broadcasted_iota(jnp.int32, (pair, pair), 0)
    cj = lax.broadcasted_iota(jnp.int32, (pair, pair), 1)
    same_head_sq = (rj < RWKV_HEAD_DIM) == (cj < RWKV_HEAD_DIM)
    for j in pairs:
        s = cols[j]
        uv = _bf(jnp.concatenate([u[j], v[:, s]], axis=0))
        bk = _bf(jnp.concatenate([b_dec[:, s], k_dec[:, s]], axis=0))
        ds = jnp.where(same_head_sq, _dot_tn(uv, bk), 0.0)
        state_ref[j] = st[j] * gamma_last[:, s] + ds

    y = jnp.concatenate(y_cols, axis=1)
    inv_n = 1.0 / RWKV_HEAD_DIM
    mean = head_sum(y) * inv_n
    yc = y - mean
    var = head_sum(yc * yc) * inv_n
    yn = yc * lax.rsqrt(var + RWKV_GN_EPS) * lnw_ref[...] + lnb_ref[...]
    bonus = head_sum(r * kmod * rk_ref[...]) * v
    y_ref[...] = ((yn + bonus) * g).astype(y_ref.dtype)


def _lane_window(start, length):
    return (start // V7X_LANES * V7X_LANES, _round_up(start + length, V7X_LANES))


def _rows_in_window(w, window, start):
    top = start - window[0]
    return jnp.pad(w, ((top, window[1] - window[0] - top - w.shape[0]), (0, 0))).astype(BF16)


def _rwkv_mix(p, mu, w0, w_w2, a0, w_a2, w_g2, k_k, k_a, r_k, ln_w, ln_b, *, batch, seq, e):
    t, width = p.shape
    lw, la, lg = w_w2.shape[0], w_a2.shape[0], w_g2.shape[0]
    assert CHUNK == RWKV_HEAD_DIM and e % (2 * RWKV_HEAD_DIM) == 0 and seq % CHUNK == 0
    win_w = _lane_window(3 * e, lw)
    win_a = _lane_window(3 * e + lw, la)
    win_g = _lane_window(3 * e + lw + la, lg)
    ww2 = _rows_in_window(w_w2, win_w, 3 * e)
    wa2 = _rows_in_window(w_a2, win_a, 3 * e + lw)
    wg2 = _rows_in_window(w_g2, win_g, 3 * e + lw + la)
    mu_p = jnp.pad(mu, (0, width - mu.shape[0])).reshape(1, width)
    row = lambda a: a.reshape(1, -1)
    nc = seq // CHUNK
    n_pairs = e // (2 * RWKV_HEAD_DIM)
    vec = lambda n: pl.BlockSpec((1, n), lambda b, i: (0, 0))
    full = lambda a: pl.BlockSpec(a.shape, lambda b, i: (0, 0))
    blk = _nbytes((CHUNK, width), F32) + _nbytes((CHUNK, e), BF16) \
        + sum(_nbytes(a.shape, a.dtype) for a in (ww2, wa2, wg2)) + 24 * _nbytes((CHUNK, e), F32)
    return pl.pallas_call(
        functools.partial(_rwkv_kernel, e=e, win_w=win_w, win_a=win_a, win_g=win_g),
        out_shape=jax.ShapeDtypeStruct((t, e), BF16),
        grid=(batch, nc),
        in_specs=[pl.BlockSpec((CHUNK, width), lambda b, i: (b * nc + i, 0)),
                  vec(width), vec(e), full(ww2), vec(e), full(wa2), full(wg2),
                  vec(e), vec(e), vec(e), vec(e), vec(e)],
        out_specs=pl.BlockSpec((CHUNK, e), lambda b, i: (b * nc + i, 0)),
        scratch_shapes=[pltpu.VMEM((V7X_SUBLANES, width), F32),
                        pltpu.VMEM((n_pairs, 2 * RWKV_HEAD_DIM, 2 * RWKV_HEAD_DIM), F32)],
        compiler_params=_params(2, blk),
        name="rwkv7_mix",
    )(p, mu_p, row(w0), ww2, row(a0), wa2, wg2, row(k_k), row(k_a), row(r_k), row(ln_w), row(ln_b))


def _gla_kernel(p_ref, go_ref, wa2_ref, ba_ref, ng_ref, y_ref, state_ref, *, heads, dk, dv):
    c = p_ref.shape[0]
    kw = heads * dk
    e = heads * dv

    @pl.when(pl.program_id(1) == 0)
    def _():
        state_ref[...] = jnp.zeros_like(state_ref)

    p = p_ref[...]
    q = p[:, 0:kw]
    k = p[:, kw:2 * kw]
    v = p[:, 2 * kw:2 * kw + e]
    ad = p[:, 2 * kw + e:2 * kw + e + wa2_ref.shape[0]]
    go = go_ref[...]

    z = _dot(_bf(ad), wa2_ref[...]) + ba_ref[...]
    gk = (jnp.minimum(z, 0.0) - jnp.log(1.0 + jnp.exp(-jnp.abs(z)))) * (1.0 / GLA_GATE_NORM)

    tri_r = lax.broadcasted_iota(jnp.int32, (c, c), 0)
    tri_c = lax.broadcasted_iota(jnp.int32, (c, c), 1)
    causal = tri_c <= tri_r
    tril = jnp.where(causal, 1.0, 0.0).astype(BF16)
    bcum = _chunk_cumsum(gk, tril)
    blast = bcum[c - 1:c, :]
    q_dec = _bf(q * (dk ** -0.5) * jnp.exp(bcum))
    k_inv = _bf(k * jnp.exp(-bcum))
    k_dec = _bf(k * jnp.exp(blast - bcum))
    chunk_decay = jnp.exp(blast)

    outs = []
    for h in range(heads):
        ks = slice(h * dk, (h + 1) * dk)
        vh = _bf(v[:, h * dv:(h + 1) * dv])
        att = jnp.where(causal, _dot_nt(q_dec[:, ks], k_inv[:, ks]), 0.0)
        st = state_ref[h]
        oh = _dot(_bf(att), vh) + _dot_nt(q_dec[:, ks], _bf(st))
        state_ref[h] = st * chunk_decay[:, ks] + _dot_tn(vh, k_dec[:, ks])
        ms = jnp.mean(oh * oh, axis=-1, keepdims=True)
        outs.append(oh * lax.rsqrt(ms + GLA_NORM_EPS) * ng_ref[...])
    o = jnp.concatenate(outs, axis=1)
    y_ref[...] = (o * (go * _sigmoid(go))).astype(y_ref.dtype)


def _gla_mix(p, go, w_a2, b_a, norm_g, *, batch, seq, heads, dk, dv):
    t, width = p.shape
    e = heads * dv
    wa2 = jnp.pad(w_a2, ((0, V7X_LANES - w_a2.shape[0]), (0, 0))).astype(BF16)
    nc = seq // CHUNK
    blk = _nbytes((CHUNK, width), F32) + _nbytes((CHUNK, e), F32) + _nbytes((CHUNK, e), BF16) \
        + _nbytes(wa2.shape, wa2.dtype) + 16 * _nbytes((CHUNK, e), F32)
    return pl.pallas_call(
        functools.partial(_gla_kernel, heads=heads, dk=dk, dv=dv),
        out_shape=jax.ShapeDtypeStruct((t, e), BF16),
        grid=(batch, nc),
        in_specs=[pl.BlockSpec((CHUNK, width), lambda b, i: (b * nc + i, 0)),
                  pl.BlockSpec((CHUNK, e), lambda b, i: (b * nc + i, 0)),
                  pl.BlockSpec(wa2.shape, lambda b, i: (0, 0)),
                  pl.BlockSpec((1, heads * dk), lambda b, i: (0, 0)),
                  pl.BlockSpec((1, dv), lambda b, i: (0, 0))],
        out_specs=pl.BlockSpec((CHUNK, e), lambda b, i: (b * nc + i, 0)),
        scratch_shapes=[pltpu.VMEM((heads, dv, dk), F32)],
        compiler_params=_params(2, blk),
        name="gla_mix",
    )(p, go, wa2, b_a.reshape(1, -1), norm_g.reshape(1, -1))


def _causal_conv3(u, prev8, w):
    return _shift_rows(u, prev8, 2) * w[0:1] + _shift_rows(u, prev8, 1) * w[1:2] + u * w[2:3]


def _sconv_kernel(bg_ref, cg_ref, hh_ref, w_ref, y_ref, carry_ref):
    @pl.when(pl.program_id(1) == 0)
    def _():
        carry_ref[...] = jnp.zeros_like(carry_ref)

    u = cg_ref[...] * hh_ref[...]
    y = bg_ref[...] * _causal_conv3(u, carry_ref[...], w_ref[...])
    carry_ref[...] = u[u.shape[0] - V7X_SUBLANES:]
    y_ref[...] = y.astype(y_ref.dtype)


def _sconv_mix(p, conv_w, *, batch, seq, e):
    t = p.shape[0]
    tm = _pick_tile(seq, 512, V7X_SUBLANES)
    ms = seq // tm
    col = lambda g: pl.BlockSpec((tm, e), lambda b, i, g=g: (b * ms + i, g))
    blk = 3 * _nbytes((tm, e), F32) + _nbytes((tm, e), BF16) + 4 * _nbytes((tm, e), F32)
    return pl.pallas_call(
        _sconv_kernel,
        out_shape=jax.ShapeDtypeStruct((t, e), BF16),
        grid=(batch, ms),
        in_specs=[col(0), col(1), col(2), pl.BlockSpec((CONV_WIDTH, e), lambda b, i: (0, 0))],
        out_specs=pl.BlockSpec((tm, e), lambda b, i: (b * ms + i, 0)),
        scratch_shapes=[pltpu.VMEM((V7X_SUBLANES, e), F32)],
        compiler_params=_params(2, blk),
        name="short_conv_mix",
    )(p, p, p, conv_w)


def _merge_kernel(ya_ref, yb_ref, yc_ref, wa_ref, wb_ref, wc_ref, ga_ref, gb_ref, gc_ref, o_ref, wbf_ref):
    @pl.when(pl.program_id(1) == 0)
    def _():
        for g, w_ref in enumerate((wa_ref, wb_ref, wc_ref)):
            _cast_rows(w_ref, wbf_ref.at[g])

    acc = ga_ref[...].astype(F32) * _dot(ya_ref[...], wbf_ref[0])
    acc = acc + gb_ref[...].astype(F32) * _dot(yb_ref[...], wbf_ref[1])
    acc = acc + gc_ref[...].astype(F32) * _dot(yc_ref[...], wbf_ref[2])
    o_ref[...] = acc.astype(o_ref.dtype)


def _merge(ya, yb, yc, w_br, layer, gates):
    t, e = ya.shape
    d = w_br.shape[-1]
    tm = _pick_tile(t, 512, V7X_SUBLANES)
    tn = _pick_tile(d, 512)
    nt = d // tn
    ysp = pl.BlockSpec((tm, e), lambda j, i: (i, 0))
    wsp = lambda g: pl.BlockSpec((None, None, e, tn), lambda j, i, g=g: (layer, g, 0, j))
    gsp = lambda g: pl.BlockSpec((tm, tn), lambda j, i, g=g: (i, g * nt + j))
    blk = 3 * _nbytes((tm, e), BF16) + 3 * _nbytes((e, tn), F32) + 3 * _nbytes((tm, tn), gates.dtype) \
        + _nbytes((tm, tn), BF16) + 2 * _nbytes((tm, tn), F32) + 3 * _nbytes((e, tn), BF16) // 2
    return pl.pallas_call(
        _merge_kernel,
        out_shape=jax.ShapeDtypeStruct((t, d), BF16),
        grid=(nt, t // tm),
        in_specs=[ysp, ysp, ysp, wsp(0), wsp(1), wsp(2), gsp(0), gsp(1), gsp(2)],
        out_specs=pl.BlockSpec((tm, tn), lambda j, i: (i, j)),
        scratch_shapes=[pltpu.VMEM((3, e, tn), BF16)],
        compiler_params=_params(2, blk),
        name="gated_merge",
    )(ya, yb, yc, w_br, w_br, w_br, gates, gates, gates)


def _ffn_up_kernel(h_ref, wg_ref, wv_ref, cg_ref, cv_ref, act_ref, wbf_ref, carry_ref):
    b = pl.program_id(1)
    m = pl.program_id(2)

    @pl.when((b == 0) & (m == 0))
    def _():
        _cast_rows(wg_ref, wbf_ref.at[0])
        _cast_rows(wv_ref, wbf_ref.at[1])

    @pl.when(m == 0)
    def _():
        carry_ref[...] = jnp.zeros_like(carry_ref)

    h = h_ref[...]
    ug = _dot(h, wbf_ref[0])
    uv = _dot(h, wbf_ref[1])
    tail = ug.shape[0] - V7X_SUBLANES
    prev_g = carry_ref[0]
    prev_v = carry_ref[1]
    carry_ref[0] = ug[tail:]
    carry_ref[1] = uv[tail:]
    xg = _causal_conv3(ug, prev_g, cg_ref[...])
    xv = _causal_conv3(uv, prev_v, cv_ref[...])
    gelu = 0.5 * xg * (1.0 + jnp.tanh(0.7978845608028654 * (xg + 0.044715 * (xg * xg * xg))))
    act_ref[...] = (gelu * xv).astype(act_ref.dtype)


def _ffn_up(h, w_up, layer, conv_w, *, batch, seq):
    t, d = h.shape
    f = w_up.shape[-1] // 2
    tm = _pick_tile(seq, 1024, V7X_SUBLANES)
    tn = _pick_tile(f, 256)
    ms, nt = seq // tm, f // tn
    blk = _nbytes((tm, d), BF16) + 2 * _nbytes((d, tn), F32) + _nbytes((tm, tn), BF16) \
        + 8 * _nbytes((tm, tn), F32) + _nbytes((d, tn), BF16)
    return pl.pallas_call(
        _ffn_up_kernel,
        out_shape=jax.ShapeDtypeStruct((t, f), BF16),
        grid=(nt, batch, ms),
        in_specs=[pl.BlockSpec((tm, d), lambda j, b, i: (b * ms + i, 0)),
                  pl.BlockSpec((None, d, tn), lambda j, b, i: (layer, 0, j)),
                  pl.BlockSpec((None, d, tn), lambda j, b, i: (layer, 0, nt + j)),
                  pl.BlockSpec((CONV_WIDTH, tn), lambda j, b, i: (0, j)),
                  pl.BlockSpec((CONV_WIDTH, tn), lambda j, b, i: (0, nt + j))],
        out_specs=pl.BlockSpec((tm, tn), lambda j, b, i: (b * ms + i, j)),
        scratch_shapes=[pltpu.VMEM((2, d, tn), BF16),
                        pltpu.VMEM((2, V7X_SUBLANES, tn), F32)],
        compiler_params=_params(3, blk),
        name="ffn_up_conv_gate",
    )(h, w_up, w_up, conv_w, conv_w)


def _layer(x, h, l, prm, next_pre_g, *, batch, seq):
    (w_in, rwkv_mu, rwkv_w0, rwkv_w_w2, rwkv_a0, rwkv_w_a2, rwkv_w_g2, rwkv_k_k, rwkv_k_a, rwkv_r_k,
     rwkv_ln_w, rwkv_ln_b, gla_w_a2, gla_b_a, gla_norm_g, sconv_w, w_br, w_o, post_mix_g, pre_ffn_g,
     w_up, ffn_conv_w, w_down, post_ffn_g) = prm
    d = x.shape[1]
    e = w_br.shape[2]
    lora = rwkv_w_w2.shape[1] + rwkv_w_a2.shape[1] + rwkv_w_g2.shape[1]
    gr, kw = gla_w_a2.shape[1:]
    dv = gla_norm_g.shape[1]
    heads = e // dv
    dk = kw // heads

    o_gla = 3 * e + lora
    o_go = o_gla + 2 * kw + e + gr
    o_conv = o_go + e
    o_gate = o_conv + 3 * e

    def in_proj(start, need, name, **kw_):
        width, tn = _tile_columns(need)
        return _proj(h, w_in, l, start, width, w_is_nk=True, tn=tn, name=name, **kw_)

    p_rwkv = in_proj(0, o_gla, "proj_rwkv", out_dtype=F32)
    p_gla = in_proj(o_gla, 2 * kw + e + V7X_LANES, "proj_gla", out_dtype=F32)
    p_go = in_proj(o_go, e, "proj_gla_gate", out_dtype=F32)
    p_conv = in_proj(o_conv, 3 * e, "proj_conv", out_dtype=F32)
    gates = in_proj(o_gate, 3 * d, "proj_gate", out_dtype=BF16, act="sigmoid")

    y_a = _rwkv_mix(p_rwkv, rwkv_mu[l], rwkv_w0[l], rwkv_w_w2[l], rwkv_a0[l], rwkv_w_a2[l], rwkv_w_g2[l],
                    rwkv_k_k[l], rwkv_k_a[l], rwkv_r_k[l], rwkv_ln_w[l], rwkv_ln_b[l],
                    batch=batch, seq=seq, e=e)
    y_b = _gla_mix(p_gla, p_go, gla_w_a2[l], gla_b_a[l], gla_norm_g[l],
                   batch=batch, seq=seq, heads=heads, dk=dk, dv=dv)
    y_c = _sconv_mix(p_conv, sconv_w[l], batch=batch, seq=seq, e=e)

    merged = _merge(y_a, y_b, y_c, w_br, l, gates)
    o = _proj(merged, w_o, l, 0, d, w_is_nk=False, tn=_tile_columns(d)[1], out_dtype=F32, name="proj_out")
    x, h2 = _residual_norm(o, x, post_mix_g[l], pre_ffn_g[l])

    act = _ffn_up(h2, w_up, l, ffn_conv_w[l], batch=batch, seq=seq)
    o2 = _matmul(act, w_down[l].astype(BF16), out_dtype=F32, tm=512, tn=512, name="ffn_down")
    return _residual_norm(o2, x, post_ffn_g[l], next_pre_g)


def kernel(x, pre_mix_g, w_in, rwkv_mu, rwkv_w0, rwkv_w_w2, rwkv_a0, rwkv_w_a2, rwkv_w_g2, rwkv_k_k, rwkv_k_a, rwkv_r_k, rwkv_ln_w, rwkv_ln_b, gla_w_a2, gla_b_a, gla_norm_g, sconv_w, w_br, w_o, post_mix_g, pre_ffn_g, w_up, ffn_conv_w, w_down, post_ffn_g):
    batch, seq, d = x.shape
    depth = w_in.shape[0]
    prm = (jnp.swapaxes(w_in, 1, 2), rwkv_mu, rwkv_w0, rwkv_w_w2, rwkv_a0, rwkv_w_a2, rwkv_w_g2, rwkv_k_k, rwkv_k_a,
           rwkv_r_k.reshape(depth, -1), rwkv_ln_w, rwkv_ln_b, gla_w_a2, gla_b_a, gla_norm_g, sconv_w,
           w_br, w_o, post_mix_g, pre_ffn_g, w_up, ffn_conv_w, w_down, post_ffn_g)
    xf = x.reshape(batch * seq, d)
    h = _rms_norm_bf16(xf, pre_mix_g[0])
    for l in range(depth):
        next_g = pre_mix_g[l + 1] if l + 1 < depth else None
        xf, h = _layer(xf, h, l, prm, next_g, batch=batch, seq=seq)
    return xf.reshape(batch, seq, d)
```

```python
import functools

import jax
import jax.numpy as jnp
from jax import lax
from jax.experimental import pallas as pl
from jax.experimental.pallas import tpu as pltpu

F32 = jnp.float32
BF16 = jnp.bfloat16

NORM_EPS = 1e-6
RWKV_HEAD_DIM = 64
RWKV_GN_EPS = 1e-5 * RWKV_HEAD_DIM
GLA_GATE_NORM = 16.0
GLA_NORM_EPS = 1e-5
CONV_WIDTH = 3
CHUNK = 64

V7X_LANES = 128
V7X_SUBLANES = 8
V7X_VMEM_BYTES = 64 * 1024 * 1024
VMEM_CAP_BYTES = V7X_VMEM_BYTES - 6 * 1024 * 1024

CAST_VREGS = 256


def _round_up(n, m):
    return (n + m - 1) // m * m


def _pick_tile(n, pref, quantum=V7X_LANES):
    if n <= pref:
        return n
    t = pref - pref % quantum
    while t >= quantum:
        if n % t == 0:
            return t
        t -= quantum
    return n


def _params(n_axes, block_bytes):
    limit = min(VMEM_CAP_BYTES, 2 * block_bytes + 16 * 1024 * 1024)
    return pltpu.CompilerParams(
        dimension_semantics=("arbitrary",) * n_axes, vmem_limit_bytes=int(limit))


def _nbytes(shape, dtype):
    n = 1
    for s in shape:
        n *= s
    return n * jnp.dtype(dtype).itemsize


def _dot(a, b):
    return jnp.dot(a, b, preferred_element_type=F32)


def _dot_nt(a, b):
    return lax.dot_general(a, b, (((1,), (1,)), ((), ())), preferred_element_type=F32)


def _dot_tn(a, b):
    return lax.dot_general(a, b, (((0,), (0,)), ((), ())), preferred_element_type=F32)


def _bf(x):
    return x.astype(BF16)


def _sigmoid(x):
    return 1.0 / (1.0 + jnp.exp(-x))


def _chunk_cumsum(x, tril_bf):
    hi = _bf(x)
    lo = _bf(x - hi.astype(F32))
    return _dot(tril_bf, hi) + _dot(tril_bf, lo)


def _shift_rows(u, prev8, k):
    r, n = u.shape
    g = r // V7X_SUBLANES
    rolled = pltpu.roll(u.reshape(g, V7X_SUBLANES, n), k, 1)
    above = jnp.concatenate([pltpu.roll(prev8, k, 0)[None], rolled[:g - 1]], axis=0)
    sub = lax.broadcasted_iota(jnp.int32, (g, V7X_SUBLANES, n), 1)
    return jnp.where(sub < k, above, rolled).reshape(r, n)


def _cast_rows(src_ref, dst_ref):
    rows, width = src_ref.shape
    step = _pick_tile(rows, max(CAST_VREGS * V7X_SUBLANES * V7X_LANES // width, 2 * V7X_SUBLANES),
                      2 * V7X_SUBLANES)
    for r in range(0, rows, step):
        dst_ref[r:r + step, :] = _bf(src_ref[r:r + step, :])


def _rms_kernel(x_ref, g_ref, h_ref):
    x = x_ref[...]
    ms = jnp.mean(x * x, axis=-1, keepdims=True)
    h_ref[...] = (x * lax.rsqrt(ms + NORM_EPS) * g_ref[...]).astype(h_ref.dtype)


def _rms_norm_bf16(x, g):
    t, d = x.shape
    tm = _pick_tile(t, 256, V7X_SUBLANES)
    blk = _nbytes((tm, d), F32) + _nbytes((tm, d), BF16)
    return pl.pallas_call(
        _rms_kernel,
        out_shape=jax.ShapeDtypeStruct((t, d), BF16),
        grid=(t // tm,),
        in_specs=[pl.BlockSpec((tm, d), lambda i: (i, 0)),
                  pl.BlockSpec((1, d), lambda i: (0, 0))],
        out_specs=pl.BlockSpec((tm, d), lambda i: (i, 0)),
        compiler_params=_params(1, blk),
        name="rms_norm",
    )(x, g.reshape(1, d))


def _residual_kernel(o_ref, x_ref, gpost_ref, *rest, with_next):
    o = o_ref[...].astype(F32)
    ms = jnp.mean(o * o, axis=-1, keepdims=True)
    xn = x_ref[...] + o * lax.rsqrt(ms + NORM_EPS) * gpost_ref[...]
    if with_next:
        gpre_ref, xo_ref, h_ref = rest
        ms2 = jnp.mean(xn * xn, axis=-1, keepdims=True)
        h_ref[...] = (xn * lax.rsqrt(ms2 + NORM_EPS) * gpre_ref[...]).astype(h_ref.dtype)
    else:
        (xo_ref,) = rest
    xo_ref[...] = xn


def _residual_norm(o, x, g_post, g_next):
    t, d = x.shape
    tm = _pick_tile(t, 256, V7X_SUBLANES)
    with_next = g_next is not None
    row = pl.BlockSpec((tm, d), lambda i: (i, 0))
    vec = pl.BlockSpec((1, d), lambda i: (0, 0))
    in_specs = [row, row, vec] + ([vec] if with_next else [])
    args = [o, x, g_post.reshape(1, d)] + ([g_next.reshape(1, d)] if with_next else [])
    out_shape = [jax.ShapeDtypeStruct((t, d), F32)]
    out_specs = [row]
    if with_next:
        out_shape.append(jax.ShapeDtypeStruct((t, d), BF16))
        out_specs.append(row)
    blk = 3 * _nbytes((tm, d), F32) + _nbytes((tm, d), BF16)
    res = pl.pallas_call(
        functools.partial(_residual_kernel, with_next=with_next),
        out_shape=out_shape,
        grid=(t // tm,),
        in_specs=in_specs,
        out_specs=out_specs,
        compiler_params=_params(1, blk),
        name="residual_norm",
    )(*args)
    return (res[0], res[1]) if with_next else (res[0], None)


def _mm_kernel(x_ref, w_ref, o_ref):
    o_ref[...] = _dot(x_ref[...], w_ref[...]).astype(o_ref.dtype)


def _matmul(x, w, *, out_dtype, tm, tn, name):
    m, k = x.shape
    _, n = w.shape
    tm = _pick_tile(m, tm, V7X_SUBLANES)
    tn = _pick_tile(n, tn)
    blk = _nbytes((tm, k), x.dtype) + _nbytes((k, tn), w.dtype) + _nbytes((tm, tn), out_dtype) \
        + _nbytes((tm, tn), F32)
    return pl.pallas_call(
        _mm_kernel,
        out_shape=jax.ShapeDtypeStruct((m, n), out_dtype),
        grid=(m // tm, n // tn),
        in_specs=[pl.BlockSpec((tm, k), lambda i, j: (i, 0)), pl.BlockSpec((k, tn), lambda i, j: (0, j))],
        out_specs=pl.BlockSpec((tm, tn), lambda i, j: (i, j)),
        compiler_params=_params(2, blk),
        name=name,
    )(x, w)


PROJ_TN_CHOICES = (1024, 768, 512)


def _tile_columns(need):
    return min(((_round_up(need, tn), tn) for tn in PROJ_TN_CHOICES), key=lambda wt: (wt[0], -wt[1]))


def _proj_kernel(x_ref, w_ref, o_ref, wbf_ref, *, w_is_nk, act):
    @pl.when(pl.program_id(1) == 0)
    def _():
        _cast_rows(w_ref, wbf_ref)

    acc = _dot_nt(x_ref[...], wbf_ref[...]) if w_is_nk else _dot(x_ref[...], wbf_ref[...])
    if act == "sigmoid":
        acc = _sigmoid(acc)
    o_ref[...] = acc.astype(o_ref.dtype)


def _proj(x, w, layer, start, width, *, w_is_nk, tn, out_dtype, act=None, name):
    m, k = x.shape
    assert width % tn == 0
    out_bytes = _nbytes((1, tn), out_dtype)
    w_bytes = _nbytes((k, tn), F32)

    def fits(t):
        need = 2 * (_nbytes((t, k), x.dtype) + w_bytes + t * out_bytes) + w_bytes // 2 + _nbytes((t, tn), F32)
        return need <= VMEM_CAP_BYTES - 4 * 1024 * 1024

    tm = next((t for t in (1024, 512, 256, 128) if m % t == 0 and fits(t)), m)
    if w_is_nk:
        assert start % V7X_SUBLANES == 0 and start + width <= w.shape[1]
        wspec = pl.BlockSpec((pl.Squeezed(), pl.Element(tn), pl.Element(k)),
                             lambda j, i: (layer, pl.multiple_of(start + j * tn, V7X_SUBLANES), 0))
        wshape = (tn, k)
    else:
        assert start % tn == 0 and start + width <= w.shape[2]
        wspec = pl.BlockSpec((None, k, tn), lambda j, i: (layer, 0, start // tn + j))
        wshape = (k, tn)
    blk = _nbytes((tm, k), x.dtype) + w_bytes + tm * out_bytes + (_nbytes((tm, tn), F32) + w_bytes // 2) // 2
    return pl.pallas_call(
        functools.partial(_proj_kernel, w_is_nk=w_is_nk, act=act),
        out_shape=jax.ShapeDtypeStruct((m, width), out_dtype),
        grid=(width // tn, m // tm),
        in_specs=[pl.BlockSpec((tm, k), lambda j, i: (i, 0)), wspec],
        out_specs=pl.BlockSpec((tm, tn), lambda j, i: (i, j)),
        scratch_shapes=[pltpu.VMEM(wshape, BF16)],
        compiler_params=_params(2, blk),
        name=name,
    )(x, w)


def _rwkv_kernel(p_ref, mu_ref, w0_ref, ww2_ref, a0_ref, wa2_ref, wg2_ref, kk_ref, ka_ref, rk_ref,
                 lnw_ref, lnb_ref, y_ref, prev_ref, state_ref, *, e, win_w, win_a, win_g):
    c = p_ref.shape[0]
    pair = 2 * RWKV_HEAD_DIM
    n_pairs = e // pair

    @pl.when(pl.program_id(1) == 0)
    def _():
        prev_ref[...] = jnp.zeros_like(prev_ref)
        state_ref[...] = jnp.zeros_like(state_ref)

    p = p_ref[...]
    shifted = _shift_rows(p, prev_ref[...], 1)
    prev_ref[...] = p[c - V7X_SUBLANES:]
    xm = p + (shifted - p) * mu_ref[...]

    r = xm[:, 0:e]
    k = xm[:, e:2 * e]
    v = xm[:, 2 * e:3 * e]
    wd = xm[:, win_w[0]:win_w[1]]
    ad = xm[:, win_a[0]:win_a[1]]
    gd = xm[:, win_g[0]:win_g[1]]

    w_pre = w0_ref[...] + _dot(_bf(jnp.tanh(wd)), ww2_ref[...])
    lw = (-0.6065306597126334) * _sigmoid(w_pre)
    a = _sigmoid(a0_ref[...] + _dot(_bf(ad), wa2_ref[...]))
    g = _dot(_bf(_sigmoid(gd)), wg2_ref[...])

    ri = lax.broadcasted_iota(jnp.int32, (pair, pair), 0)
    ci = lax.broadcasted_iota(jnp.int32, (pair, pair), 1)
    same_head = (ri < RWKV_HEAD_DIM) == (ci < RWKV_HEAD_DIM)
    head_ones = jnp.where(same_head, 1.0, 0.0).astype(BF16)

    def head_sum(x):
        xb = _bf(x)
        return jnp.concatenate(
            [_dot(xb[:, j * pair:(j + 1) * pair], head_ones) for j in range(n_pairs)], axis=1)

    kk = k * kk_ref[...]
    kmod = k * (1.0 + (a - 1.0) * ka_ref[...])
    sums = head_sum(jnp.concatenate([kk * kk, r * kmod * rk_ref[...]], axis=0))
    kk = kk / jnp.maximum(jnp.sqrt(sums[:c]), 1e-12)
    bonus_scale = sums[c:]
    bvec = kk * a

    tri_r = lax.broadcasted_iota(jnp.int32, (c, c), 0)
    tri_c = lax.broadcasted_iota(jnp.int32, (c, c), 1)
    tril = jnp.where(tri_c <= tri_r, 1.0, 0.0).astype(BF16)
    gcum = _chunk_cumsum(lw, tril)
    glast = gcum[c - 1:c, :]
    e_pos = jnp.exp(gcum)
    e_neg = jnp.exp(-gcum)
    e_tail = jnp.exp(glast - gcum)
    r_t = r * e_pos
    kap_t = kk * jnp.exp(gcum - lw)
    b_t = bvec * e_neg
    k_t = kmod * e_neg
    b_dec = bvec * e_tail
    k_dec = kmod * e_tail
    gamma_last = jnp.exp(glast)

    wt = lax.broadcasted_iota(jnp.int32, (c, pair), 0)
    wl = lax.broadcasted_iota(jnp.int32, (c, pair), 1)
    ws = wl & (RWKV_HEAD_DIM - 1)
    lane_lo = wl < RWKV_HEAD_DIM
    m_strict = ws < wt
    m_incl = ws <= wt
    m_16 = m_strict & ((ws >> 4) == (wt >> 4))
    m_32 = m_strict & ((ws >> 5) == (wt >> 5)) & ((ws >> 4) != (wt >> 4))
    m_64 = m_strict & ((ws >> 5) != (wt >> 5))
    eye_w = jnp.where(ws == wt, 1.0, 0.0).astype(F32)
    zero_b = jnp.zeros((c, pair), BF16)

    def stack2(x):
        xb = _bf(x)
        return jnp.concatenate([jnp.where(lane_lo, xb, zero_b), jnp.where(lane_lo, zero_b, xb)], axis=0)

    pairs = range(n_pairs)
    cols = [slice(j * pair, (j + 1) * pair) for j in pairs]
    lhs = [_bf(jnp.concatenate([kap_t[:, s], r_t[:, s]], axis=0)) for s in cols]
    bk_s = [jnp.concatenate([stack2(b_t[:, s]), stack2(k_t[:, s])], axis=0) for s in cols]
    v_s = [stack2(v[:, s]) for s in cols]
    scores = [_dot_nt(lhs[j], bk_s[j]) for j in pairs]
    st = [state_ref[j] for j in pairs]
    proj = [_dot_nt(lhs[j], _bf(st[j])) for j in pairs]
    a_w = [jnp.where(m_strict, scores[j][:c, :pair], 0.0) for j in pairs]
    b_w = [_bf(jnp.where(m_strict, scores[j][:c, pair:], 0.0)) for j in pairs]
    r_w = [_bf(jnp.concatenate([jnp.where(m_incl, scores[j][c:, :pair], 0.0),
                                jnp.where(m_incl, scores[j][c:, pair:], 0.0)], axis=1)) for j in pairs]
    rhs_u = [-(proj[j][:c] + _dot(b_w[j], v_s[j])) for j in pairs]

    d1 = [jnp.where(m_16, a_w[j], 0.0) for j in pairs]
    e32 = [stack2(jnp.where(m_32, a_w[j], 0.0)) for j in pairs]
    e64 = [_bf(jnp.where(m_64, a_w[j], 0.0)) for j in pairs]
    t_inv = [eye_w - d1[j] for j in pairs]
    d2 = [_dot(_bf(d1[j]), stack2(d1[j])) for j in pairs]
    nxt = [_dot(_bf(jnp.concatenate([t_inv[j], d2[j]], axis=0)), stack2(d2[j])) for j in pairs]
    t_inv = [t_inv[j] + nxt[j][:c] for j in pairs]
    d4 = [nxt[j][c:] for j in pairs]
    nxt = [_dot(_bf(jnp.concatenate([t_inv[j], d4[j]], axis=0)), stack2(d4[j])) for j in pairs]
    t_inv = [t_inv[j] + nxt[j][:c] for j in pairs]
    t_inv = [t_inv[j] + _dot(_bf(t_inv[j]), stack2(nxt[j][c:])) for j in pairs]
    te = [_dot(_bf(t_inv[j]), e32[j]) for j in pairs]
    t32 = [_bf(t_inv[j] - _dot(_bf(te[j]), stack2(t_inv[j]))) for j in pairs]
    z = [_dot(t32[j], stack2(rhs_u[j])) for j in pairs]
    ez = [_dot(e64[j], stack2(z[j])) for j in pairs]
    u = [z[j] - _dot(t32[j], stack2(ez[j])) for j in pairs]

    uv_s = [jnp.concatenate([stack2(u[j]), v_s[j]], axis=0) for j in pairs]
    y_cols = [proj[j][c:] + _dot(r_w[j], uv_s[j]) for j in pairs]

    rj = lax.broadcasted_iota(jnp.int32, (pair, pair), 0)
    cj = lax.broadcasted_iota(jnp.int32, (pair, pair), 1)
    same_head_sq = (rj < RWKV_HEAD_DIM) == (cj < RWKV_HEAD_DIM)
    for j in pairs:
        s = cols[j]
        uv = _bf(jnp.concatenate([u[j], v[:, s]], axis=0))
        bk = _bf(jnp.concatenate([b_dec[:, s], k_dec[:, s]], axis=0))
        ds = jnp.where(same_head_sq, _dot_tn(uv, bk), 0.0)
        state_ref[j] = st[j] * gamma_last[:, s] + ds

    y = jnp.concatenate(y_cols, axis=1)
    inv_n = 1.0 / RWKV_HEAD_DIM
    mean = head_sum(y) * inv_n
    yc = y - mean
    var = head_sum(yc * yc) * inv_n
    yn = yc * lax.rsqrt(var + RWKV_GN_EPS) * lnw_ref[...] + lnb_ref[...]
    y_ref[...] = ((yn + bonus_scale * v) * g).astype(y_ref.dtype)


def _lane_window(start, length):
    return (start // V7X_LANES * V7X_LANES, _round_up(start + length, V7X_LANES))


def _rows_in_window(w, window, start):
    top = start - window[0]
    return jnp.pad(w, ((top, window[1] - window[0] - top - w.shape[0]), (0, 0))).astype(BF16)


def _rwkv_mix(p, mu, w0, w_w2, a0, w_a2, w_g2, k_k, k_a, r_k, ln_w, ln_b, *, batch, seq, e):
    t, width = p.shape
    lw, la, lg = w_w2.shape[0], w_a2.shape[0], w_g2.shape[0]
    assert CHUNK == RWKV_HEAD_DIM and e % (2 * RWKV_HEAD_DIM) == 0 and seq % CHUNK == 0
    win_w = _lane_window(3 * e, lw)
    win_a = _lane_window(3 * e + lw, la)
    win_g = _lane_window(3 * e + lw + la, lg)
    ww2 = _rows_in_window(w_w2, win_w, 3 * e)
    wa2 = _rows_in_window(w_a2, win_a, 3 * e + lw)
    wg2 = _rows_in_window(w_g2, win_g, 3 * e + lw + la)
    mu_p = jnp.pad(mu, (0, width - mu.shape[0])).reshape(1, width)
    row = lambda a: a.reshape(1, -1)
    nc = seq // CHUNK
    n_pairs = e // (2 * RWKV_HEAD_DIM)
    vec = lambda n: pl.BlockSpec((1, n), lambda b, i: (0, 0))
    full = lambda a: pl.BlockSpec(a.shape, lambda b, i: (0, 0))
    blk = _nbytes((CHUNK, width), F32) + _nbytes((CHUNK, e), BF16) \
        + sum(_nbytes(a.shape, a.dtype) for a in (ww2, wa2, wg2)) + 24 * _nbytes((CHUNK, e), F32)
    return pl.pallas_call(
        functools.partial(_rwkv_kernel, e=e, win_w=win_w, win_a=win_a, win_g=win_g),
        out_shape=jax.ShapeDtypeStruct((t, e), BF16),
        grid=(batch, nc),
        in_specs=[pl.BlockSpec((CHUNK, width), lambda b, i: (b * nc + i, 0)),
                  vec(width), vec(e), full(ww2), vec(e), full(wa2), full(wg2),
                  vec(e), vec(e), vec(e), vec(e), vec(e)],
        out_specs=pl.BlockSpec((CHUNK, e), lambda b, i: (b * nc + i, 0)),
        scratch_shapes=[pltpu.VMEM((V7X_SUBLANES, width), F32),
                        pltpu.VMEM((n_pairs, 2 * RWKV_HEAD_DIM, 2 * RWKV_HEAD_DIM), F32)],
        compiler_params=_params(2, blk),
        name="rwkv7_mix",
    )(p, mu_p, row(w0), ww2, row(a0), wa2, wg2, row(k_k), row(k_a), row(r_k), row(ln_w), row(ln_b))


def _gla_kernel(p_ref, go_ref, wa2_ref, ba_ref, ng_ref, y_ref, state_ref, *, heads, dk, dv):
    c = p_ref.shape[0]
    kw = heads * dk
    e = heads * dv

    @pl.when(pl.program_id(1) == 0)
    def _():
        state_ref[...] = jnp.zeros_like(state_ref)

    p = p_ref[...]
    q = p[:, 0:kw]
    k = p[:, kw:2 * kw]
    v = p[:, 2 * kw:2 * kw + e]
    ad = p[:, 2 * kw + e:2 * kw + e + wa2_ref.shape[0]]
    go = go_ref[...]

    z = _dot(_bf(ad), wa2_ref[...]) + ba_ref[...]
    gk = (jnp.minimum(z, 0.0) - jnp.log(1.0 + jnp.exp(-jnp.abs(z)))) * (1.0 / GLA_GATE_NORM)

    tri_r = lax.broadcasted_iota(jnp.int32, (c, c), 0)
    tri_c = lax.broadcasted_iota(jnp.int32, (c, c), 1)
    causal = tri_c <= tri_r
    tril = jnp.where(causal, 1.0, 0.0).astype(BF16)
    bcum = _chunk_cumsum(gk, tril)
    blast = bcum[c - 1:c, :]
    q_dec = _bf(q * (dk ** -0.5) * jnp.exp(bcum))
    k_inv = _bf(k * jnp.exp(-bcum))
    k_dec = _bf(k * jnp.exp(blast - bcum))
    chunk_decay = jnp.exp(blast)

    outs = []
    for h in range(heads):
        ks = slice(h * dk, (h + 1) * dk)
        vh = _bf(v[:, h * dv:(h + 1) * dv])
        att = jnp.where(causal, _dot_nt(q_dec[:, ks], k_inv[:, ks]), 0.0)
        st = state_ref[h]
        oh = _dot(_bf(att), vh) + _dot_nt(q_dec[:, ks], _bf(st))
        state_ref[h] = st * chunk_decay[:, ks] + _dot_tn(vh, k_dec[:, ks])
        ms = jnp.mean(oh * oh, axis=-1, keepdims=True)
        outs.append(oh * lax.rsqrt(ms + GLA_NORM_EPS) * ng_ref[...])
    o = jnp.concatenate(outs, axis=1)
    y_ref[...] = (o * (go * _sigmoid(go))).astype(y_ref.dtype)


def _gla_mix(p, go, w_a2, b_a, norm_g, *, batch, seq, heads, dk, dv):
    t, width = p.shape
    e = heads * dv
    wa2 = jnp.pad(w_a2, ((0, V7X_LANES - w_a2.shape[0]), (0, 0))).astype(BF16)
    nc = seq // CHUNK
    blk = _nbytes((CHUNK, width), F32) + _nbytes((CHUNK, e), F32) + _nbytes((CHUNK, e), BF16) \
        + _nbytes(wa2.shape, wa2.dtype) + 16 * _nbytes((CHUNK, e), F32)
    return pl.pallas_call(
        functools.partial(_gla_kernel, heads=heads, dk=dk, dv=dv),
        out_shape=jax.ShapeDtypeStruct((t, e), BF16),
        grid=(batch, nc),
        in_specs=[pl.BlockSpec((CHUNK, width), lambda b, i: (b * nc + i, 0)),
                  pl.BlockSpec((CHUNK, e), lambda b, i: (b * nc + i, 0)),
                  pl.BlockSpec(wa2.shape, lambda b, i: (0, 0)),
                  pl.BlockSpec((1, heads * dk), lambda b, i: (0, 0)),
                  pl.BlockSpec((1, dv), lambda b, i: (0, 0))],
        out_specs=pl.BlockSpec((CHUNK, e), lambda b, i: (b * nc + i, 0)),
        scratch_shapes=[pltpu.VMEM((heads, dv, dk), F32)],
        compiler_params=_params(2, blk),
        name="gla_mix",
    )(p, go, wa2, b_a.reshape(1, -1), norm_g.reshape(1, -1))


def _causal_conv3(u, prev8, w):
    return _shift_rows(u, prev8, 2) * w[0:1] + _shift_rows(u, prev8, 1) * w[1:2] + u * w[2:3]


def _sconv_kernel(bg_ref, cg_ref, hh_ref, w_ref, y_ref, carry_ref):
    @pl.when(pl.program_id(1) == 0)
    def _():
        carry_ref[...] = jnp.zeros_like(carry_ref)

    u = cg_ref[...] * hh_ref[...]
    y = bg_ref[...] * _causal_conv3(u, carry_ref[...], w_ref[...])
    carry_ref[...] = u[u.shape[0] - V7X_SUBLANES:]
    y_ref[...] = y.astype(y_ref.dtype)


def _sconv_mix(p, conv_w, *, batch, seq, e):
    t = p.shape[0]
    tm = _pick_tile(seq, 512, V7X_SUBLANES)
    ms = seq // tm
    col = lambda g: pl.BlockSpec((tm, e), lambda b, i, g=g: (b * ms + i, g))
    blk = 3 * _nbytes((tm, e), F32) + _nbytes((tm, e), BF16) + 4 * _nbytes((tm, e), F32)
    return pl.pallas_call(
        _sconv_kernel,
        out_shape=jax.ShapeDtypeStruct((t, e), BF16),
        grid=(batch, ms),
        in_specs=[col(0), col(1), col(2), pl.BlockSpec((CONV_WIDTH, e), lambda b, i: (0, 0))],
        out_specs=pl.BlockSpec((tm, e), lambda b, i: (b * ms + i, 0)),
        scratch_shapes=[pltpu.VMEM((V7X_SUBLANES, e), F32)],
        compiler_params=_params(2, blk),
        name="short_conv_mix",
    )(p, p, p, conv_w)


def _merge_kernel(ya_ref, yb_ref, yc_ref, wa_ref, wb_ref, wc_ref, ga_ref, gb_ref, gc_ref, o_ref, wbf_ref):
    @pl.when(pl.program_id(1) == 0)
    def _():
        for g, w_ref in enumerate((wa_ref, wb_ref, wc_ref)):
            _cast_rows(w_ref, wbf_ref.at[g])

    acc = ga_ref[...].astype(F32) * _dot(ya_ref[...], wbf_ref[0])
    acc = acc + gb_ref[...].astype(F32) * _dot(yb_ref[...], wbf_ref[1])
    acc = acc + gc_ref[...].astype(F32) * _dot(yc_ref[...], wbf_ref[2])
    o_ref[...] = acc.astype(o_ref.dtype)


def _merge(ya, yb, yc, w_br, layer, gates):
    t, e = ya.shape
    d = w_br.shape[-1]
    tm = _pick_tile(t, 512, V7X_SUBLANES)
    tn = _pick_tile(d, 512)
    nt = d // tn
    ysp = pl.BlockSpec((tm, e), lambda j, i: (i, 0))
    wsp = lambda g: pl.BlockSpec((None, None, e, tn), lambda j, i, g=g: (layer, g, 0, j))
    gsp = lambda g: pl.BlockSpec((tm, tn), lambda j, i, g=g: (i, g * nt + j))
    blk = 3 * _nbytes((tm, e), BF16) + 3 * _nbytes((e, tn), F32) + 3 * _nbytes((tm, tn), gates.dtype) \
        + _nbytes((tm, tn), BF16) + 2 * _nbytes((tm, tn), F32) + 3 * _nbytes((e, tn), BF16) // 2
    return pl.pallas_call(
        _merge_kernel,
        out_shape=jax.ShapeDtypeStruct((t, d), BF16),
        grid=(nt, t // tm),
        in_specs=[ysp, ysp, ysp, wsp(0), wsp(1), wsp(2), gsp(0), gsp(1), gsp(2)],
        out_specs=pl.BlockSpec((tm, tn), lambda j, i: (i, j)),
        scratch_shapes=[pltpu.VMEM((3, e, tn), BF16)],
        compiler_params=_params(2, blk),
        name="gated_merge",
    )(ya, yb, yc, w_br, w_br, w_br, gates, gates, gates)


def _ffn_up_kernel(h_ref, wg_ref, wv_ref, cg_ref, cv_ref, act_ref, wbf_ref, carry_ref):
    b = pl.program_id(1)
    m = pl.program_id(2)

    @pl.when((b == 0) & (m == 0))
    def _():
        _cast_rows(wg_ref, wbf_ref.at[0])
        _cast_rows(wv_ref, wbf_ref.at[1])

    @pl.when(m == 0)
    def _():
        carry_ref[...] = jnp.zeros_like(carry_ref)

    h = h_ref[...]
    ug = _dot(h, wbf_ref[0])
    uv = _dot(h, wbf_ref[1])
    tail = ug.shape[0] - V7X_SUBLANES
    prev_g = carry_ref[0]
    prev_v = carry_ref[1]
    carry_ref[0] = ug[tail:]
    carry_ref[1] = uv[tail:]
    xg = _causal_conv3(ug, prev_g, cg_ref[...])
    xv = _causal_conv3(uv, prev_v, cv_ref[...])
    gelu = 0.5 * xg * (1.0 + jnp.tanh(0.7978845608028654 * (xg + 0.044715 * (xg * xg * xg))))
    act_ref[...] = (gelu * xv).astype(act_ref.dtype)


def _ffn_up(h, w_up, layer, conv_w, *, batch, seq):
    t, d = h.shape
    f = w_up.shape[-1] // 2
    tm = _pick_tile(seq, 1024, V7X_SUBLANES)
    tn = _pick_tile(f, 256)
    ms, nt = seq // tm, f // tn
    blk = _nbytes((tm, d), BF16) + 2 * _nbytes((d, tn), F32) + _nbytes((tm, tn), BF16) \
        + 8 * _nbytes((tm, tn), F32) + _nbytes((d, tn), BF16)
    return pl.pallas_call(
        _ffn_up_kernel,
        out_shape=jax.ShapeDtypeStruct((t, f), BF16),
        grid=(nt, batch, ms),
        in_specs=[pl.BlockSpec((tm, d), lambda j, b, i: (b * ms + i, 0)),
                  pl.BlockSpec((None, d, tn), lambda j, b, i: (layer, 0, j)),
                  pl.BlockSpec((None, d, tn), lambda j, b, i: (layer, 0, nt + j)),
                  pl.BlockSpec((CONV_WIDTH, tn), lambda j, b, i: (0, j)),
                  pl.BlockSpec((CONV_WIDTH, tn), lambda j, b, i: (0, nt + j))],
        out_specs=pl.BlockSpec((tm, tn), lambda j, b, i: (b * ms + i, j)),
        scratch_shapes=[pltpu.VMEM((2, d, tn), BF16),
                        pltpu.VMEM((2, V7X_SUBLANES, tn), F32)],
        compiler_params=_params(3, blk),
        name="ffn_up_conv_gate",
    )(h, w_up, w_up, conv_w, conv_w)


def _layer(x, h, l, prm, next_pre_g, *, batch, seq):
    (w_in, rwkv_mu, rwkv_w0, rwkv_w_w2, rwkv_a0, rwkv_w_a2, rwkv_w_g2, rwkv_k_k, rwkv_k_a, rwkv_r_k,
     rwkv_ln_w, rwkv_ln_b, gla_w_a2, gla_b_a, gla_norm_g, sconv_w, w_br, w_o, post_mix_g, pre_ffn_g,
     w_up, ffn_conv_w, w_down, post_ffn_g) = prm
    d = x.shape[1]
    e = w_br.shape[2]
    lora = rwkv_w_w2.shape[1] + rwkv_w_a2.shape[1] + rwkv_w_g2.shape[1]
    gr, kw = gla_w_a2.shape[1:]
    dv = gla_norm_g.shape[1]
    heads = e // dv
    dk = kw // heads

    o_gla = 3 * e + lora
    o_go = o_gla + 2 * kw + e + gr
    o_conv = o_go + e
    o_gate = o_conv + 3 * e

    def in_proj(start, need, name, **kw_):
        width, tn = _tile_columns(need)
        return _proj(h, w_in, l, start, width, w_is_nk=True, tn=tn, name=name, **kw_)

    p_rwkv = in_proj(0, o_gla, "proj_rwkv", out_dtype=F32)
    p_gla = in_proj(o_gla, 2 * kw + e + V7X_LANES, "proj_gla", out_dtype=F32)
    p_go = in_proj(o_go, e, "proj_gla_gate", out_dtype=F32)
    p_conv = in_proj(o_conv, 3 * e, "proj_conv", out_dtype=F32)
    gates = in_proj(o_gate, 3 * d, "proj_gate", out_dtype=BF16, act="sigmoid")

    y_a = _rwkv_mix(p_rwkv, rwkv_mu[l], rwkv_w0[l], rwkv_w_w2[l], rwkv_a0[l], rwkv_w_a2[l], rwkv_w_g2[l],
                    rwkv_k_k[l], rwkv_k_a[l], rwkv_r_k[l], rwkv_ln_w[l], rwkv_ln_b[l],
                    batch=batch, seq=seq, e=e)
    y_b = _gla_mix(p_gla, p_go, gla_w_a2[l], gla_b_a[l], gla_norm_g[l],
                   batch=batch, seq=seq, heads=heads, dk=dk, dv=dv)
    y_c = _sconv_mix(p_conv, sconv_w[l], batch=batch, seq=seq, e=e)

    merged = _merge(y_a, y_b, y_c, w_br, l, gates)
    o = _proj(merged, w_o, l, 0, d, w_is_nk=False, tn=_tile_columns(d)[1], out_dtype=BF16, name="proj_out")
    x, h2 = _residual_norm(o, x, post_mix_g[l], pre_ffn_g[l])

    act = _ffn_up(h2, w_up, l, ffn_conv_w[l], batch=batch, seq=seq)
    o2 = _matmul(act, w_down[l].astype(BF16), out_dtype=BF16, tm=512, tn=512, name="ffn_down")
    return _residual_norm(o2, x, post_ffn_g[l], next_pre_g)


def kernel(x, pre_mix_g, w_in, rwkv_mu, rwkv_w0, rwkv_w_w2, rwkv_a0, rwkv_w_a2, rwkv_w_g2, rwkv_k_k, rwkv_k_a, rwkv_r_k, rwkv_ln_w, rwkv_ln_b, gla_w_a2, gla_b_a, gla_norm_g, sconv_w, w_br, w_o, post_mix_g, pre_ffn_g, w_up, ffn_conv_w, w_down, post_ffn_g):
    batch, seq, d = x.shape
    depth = w_in.shape[0]
    prm = (jnp.swapaxes(w_in, 1, 2), rwkv_mu, rwkv_w0, rwkv_w_w2, rwkv_a0, rwkv_w_a2, rwkv_w_g2, rwkv_k_k, rwkv_k_a,
           rwkv_r_k.reshape(depth, -1), rwkv_ln_w, rwkv_ln_b, gla_w_a2, gla_b_a, gla_norm_g, sconv_w,
           w_br, w_o, post_mix_g, pre_ffn_g, w_up, ffn_conv_w, w_down, post_ffn_g)
    xf = x.reshape(batch * seq, d)
    h = _rms_norm_bf16(xf, pre_mix_g[0])
    for l in range(depth):
        next_g = pre_mix_g[l + 1] if l + 1 < depth else None
        xf, h = _layer(xf, h, l, prm, next_g, batch=batch, seq=seq)
    return xf.reshape(batch, seq, d)
```

```python
import functools

import jax
import jax.numpy as jnp
from jax import lax
from jax.experimental import pallas as pl
from jax.experimental.pallas import tpu as pltpu

F32 = jnp.float32
BF16 = jnp.bfloat16

NORM_EPS = 1e-6
RWKV_HEAD_DIM = 64
RWKV_GN_EPS = 1e-5 * RWKV_HEAD_DIM
GLA_GATE_NORM = 16.0
GLA_NORM_EPS = 1e-5
CONV_WIDTH = 3
CHUNK = 64

V7X_LANES = 128
V7X_SUBLANES = 8
V7X_VMEM_BYTES = 64 * 1024 * 1024
VMEM_CAP_BYTES = V7X_VMEM_BYTES - 6 * 1024 * 1024

CAST_VREGS = 256


def _round_up(n, m):
    return (n + m - 1) // m * m


def _pick_tile(n, pref, quantum=V7X_LANES):
    if n <= pref:
        return n
    t = pref - pref % quantum
    while t >= quantum:
        if n % t == 0:
            return t
        t -= quantum
    return n


def _params(n_axes, block_bytes):
    limit = min(VMEM_CAP_BYTES, 2 * block_bytes + 16 * 1024 * 1024)
    return pltpu.CompilerParams(
        dimension_semantics=("arbitrary",) * n_axes, vmem_limit_bytes=int(limit))


def _nbytes(shape, dtype):
    n = 1
    for s in shape:
        n *= s
    return n * jnp.dtype(dtype).itemsize


def _dot(a, b):
    return jnp.dot(a, b, preferred_element_type=F32)


def _dot_nt(a, b):
    return lax.dot_general(a, b, (((1,), (1,)), ((), ())), preferred_element_type=F32)


def _dot_tn(a, b):
    return lax.dot_general(a, b, (((0,), (0,)), ((), ())), preferred_element_type=F32)


def _bf(x):
    return x.astype(BF16)


def _sigmoid(x):
    return 1.0 / (1.0 + jnp.exp(-x))


def _chunk_cumsum(x, tril_bf):
    hi = _bf(x)
    lo = _bf(x - hi.astype(F32))
    return _dot(tril_bf, hi) + _dot(tril_bf, lo)


def _shift_rows(u, prev8, k):
    r, n = u.shape
    g = r // V7X_SUBLANES
    rolled = pltpu.roll(u.reshape(g, V7X_SUBLANES, n), k, 1)
    above = jnp.concatenate([pltpu.roll(prev8, k, 0)[None], rolled[:g - 1]], axis=0)
    sub = lax.broadcasted_iota(jnp.int32, (g, V7X_SUBLANES, n), 1)
    return jnp.where(sub < k, above, rolled).reshape(r, n)


def _cast_rows(src_ref, dst_ref):
    rows, width = src_ref.shape
    step = _pick_tile(rows, max(CAST_VREGS * V7X_SUBLANES * V7X_LANES // width, 2 * V7X_SUBLANES),
                      2 * V7X_SUBLANES)
    for r in range(0, rows, step):
        dst_ref[r:r + step, :] = _bf(src_ref[r:r + step, :])


def _rms_kernel(x_ref, g_ref, h_ref):
    x = x_ref[...]
    ms = jnp.mean(x * x, axis=-1, keepdims=True)
    h_ref[...] = (x * lax.rsqrt(ms + NORM_EPS) * g_ref[...]).astype(h_ref.dtype)


def _rms_norm_bf16(x, g):
    t, d = x.shape
    tm = _pick_tile(t, 256, V7X_SUBLANES)
    blk = _nbytes((tm, d), F32) + _nbytes((tm, d), BF16)
    return pl.pallas_call(
        _rms_kernel,
        out_shape=jax.ShapeDtypeStruct((t, d), BF16),
        grid=(t // tm,),
        in_specs=[pl.BlockSpec((tm, d), lambda i: (i, 0)),
                  pl.BlockSpec((1, d), lambda i: (0, 0))],
        out_specs=pl.BlockSpec((tm, d), lambda i: (i, 0)),
        compiler_params=_params(1, blk),
        name="rms_norm",
    )(x, g.reshape(1, d))


def _residual_kernel(o_ref, x_ref, gpost_ref, *rest, with_next):
    o = o_ref[...].astype(F32)
    ms = jnp.mean(o * o, axis=-1, keepdims=True)
    xn = x_ref[...] + o * lax.rsqrt(ms + NORM_EPS) * gpost_ref[...]
    if with_next:
        gpre_ref, xo_ref, h_ref = rest
        ms2 = jnp.mean(xn * xn, axis=-1, keepdims=True)
        h_ref[...] = (xn * lax.rsqrt(ms2 + NORM_EPS) * gpre_ref[...]).astype(h_ref.dtype)
    else:
        (xo_ref,) = rest
    xo_ref[...] = xn


def _residual_norm(o, x, g_post, g_next):
    t, d = x.shape
    tm = _pick_tile(t, 256, V7X_SUBLANES)
    with_next = g_next is not None
    row = pl.BlockSpec((tm, d), lambda i: (i, 0))
    vec = pl.BlockSpec((1, d), lambda i: (0, 0))
    in_specs = [row, row, vec] + ([vec] if with_next else [])
    args = [o, x, g_post.reshape(1, d)] + ([g_next.reshape(1, d)] if with_next else [])
    out_shape = [jax.ShapeDtypeStruct((t, d), F32)]
    out_specs = [row]
    if with_next:
        out_shape.append(jax.ShapeDtypeStruct((t, d), BF16))
        out_specs.append(row)
    blk = 3 * _nbytes((tm, d), F32) + _nbytes((tm, d), BF16)
    res = pl.pallas_call(
        functools.partial(_residual_kernel, with_next=with_next),
        out_shape=out_shape,
        grid=(t // tm,),
        in_specs=in_specs,
        out_specs=out_specs,
        compiler_params=_params(1, blk),
        name="residual_norm",
    )(*args)
    return (res[0], res[1]) if with_next else (res[0], None)


def _mm_kernel(x_ref, w_ref, o_ref):
    o_ref[...] = _dot(x_ref[...], w_ref[...]).astype(o_ref.dtype)


def _matmul(x, w, *, out_dtype, tm, tn, name):
    m, k = x.shape
    _, n = w.shape
    tm = _pick_tile(m, tm, V7X_SUBLANES)
    tn = _pick_tile(n, tn)
    blk = _nbytes((tm, k), x.dtype) + _nbytes((k, tn), w.dtype) + _nbytes((tm, tn), out_dtype) \
        + _nbytes((tm, tn), F32)
    return pl.pallas_call(
        _mm_kernel,
        out_shape=jax.ShapeDtypeStruct((m, n), out_dtype),
        grid=(m // tm, n // tn),
        in_specs=[pl.BlockSpec((tm, k), lambda i, j: (i, 0)), pl.BlockSpec((k, tn), lambda i, j: (0, j))],
        out_specs=pl.BlockSpec((tm, tn), lambda i, j: (i, j)),
        compiler_params=_params(2, blk),
        name=name,
    )(x, w)


PROJ_TN_CHOICES = (1024, 768, 512)


def _tile_columns(need):
    return min(((_round_up(need, tn), tn) for tn in PROJ_TN_CHOICES), key=lambda wt: (wt[0], -wt[1]))


def _proj_kernel(x_ref, w_ref, o_ref, wbf_ref, *, w_is_nk, act):
    @pl.when(pl.program_id(1) == 0)
    def _():
        _cast_rows(w_ref, wbf_ref)

    acc = _dot_nt(x_ref[...], wbf_ref[...]) if w_is_nk else _dot(x_ref[...], wbf_ref[...])
    if act == "sigmoid":
        acc = _sigmoid(acc)
    o_ref[...] = acc.astype(o_ref.dtype)


def _proj(x, w, layer, start, width, *, w_is_nk, tn, out_dtype, act=None, name):
    m, k = x.shape
    assert width % tn == 0
    out_bytes = _nbytes((1, tn), out_dtype)
    w_bytes = _nbytes((k, tn), F32)

    def fits(t):
        need = 2 * (_nbytes((t, k), x.dtype) + w_bytes + t * out_bytes) + w_bytes // 2 + _nbytes((t, tn), F32)
        return need <= VMEM_CAP_BYTES - 4 * 1024 * 1024

    tm = next((t for t in (1024, 512, 256, 128) if m % t == 0 and fits(t)), m)
    if w_is_nk:
        assert start % V7X_SUBLANES == 0 and start + width <= w.shape[1]
        wspec = pl.BlockSpec((pl.Squeezed(), pl.Element(tn), pl.Element(k)),
                             lambda j, i: (layer, pl.multiple_of(start + j * tn, V7X_SUBLANES), 0))
        wshape = (tn, k)
    else:
        assert start % tn == 0 and start + width <= w.shape[2]
        wspec = pl.BlockSpec((None, k, tn), lambda j, i: (layer, 0, start // tn + j))
        wshape = (k, tn)
    blk = _nbytes((tm, k), x.dtype) + w_bytes + tm * out_bytes + (_nbytes((tm, tn), F32) + w_bytes // 2) // 2
    return pl.pallas_call(
        functools.partial(_proj_kernel, w_is_nk=w_is_nk, act=act),
        out_shape=jax.ShapeDtypeStruct((m, width), out_dtype),
        grid=(width // tn, m // tm),
        in_specs=[pl.BlockSpec((tm, k), lambda j, i: (i, 0)), wspec],
        out_specs=pl.BlockSpec((tm, tn), lambda j, i: (i, j)),
        scratch_shapes=[pltpu.VMEM(wshape, BF16)],
        compiler_params=_params(2, blk),
        name=name,
    )(x, w)


def _rwkv_kernel(p_ref, mu_ref, w0_ref, ww2_ref, a0_ref, wa2_ref, wg2_ref, kk_ref, ka_ref, rk_ref,
                 lnw_ref, lnb_ref, y_ref, prev_ref, state_ref, *, e, win_w, win_a, win_g):
    c = p_ref.shape[0]
    pair = 2 * RWKV_HEAD_DIM
    n_pairs = e // pair

    @pl.when(pl.program_id(1) == 0)
    def _():
        prev_ref[...] = jnp.zeros_like(prev_ref)
        state_ref[...] = jnp.zeros_like(state_ref)

    p = p_ref[...]
    shifted = _shift_rows(p, prev_ref[...], 1)
    prev_ref[...] = p[c - V7X_SUBLANES:]
    xm = p + (shifted - p) * mu_ref[...]

    r = xm[:, 0:e]
    k = xm[:, e:2 * e]
    v = xm[:, 2 * e:3 * e]
    wd = xm[:, win_w[0]:win_w[1]]
    ad = xm[:, win_a[0]:win_a[1]]
    gd = xm[:, win_g[0]:win_g[1]]

    w_pre = w0_ref[...] + _dot(_bf(jnp.tanh(wd)), ww2_ref[...])
    lw = (-0.6065306597126334) * _sigmoid(w_pre)
    a = _sigmoid(a0_ref[...] + _dot(_bf(ad), wa2_ref[...]))
    g = _dot(_bf(_sigmoid(gd)), wg2_ref[...])

    ri = lax.broadcasted_iota(jnp.int32, (pair, pair), 0)
    ci = lax.broadcasted_iota(jnp.int32, (pair, pair), 1)
    same_head = (ri < RWKV_HEAD_DIM) == (ci < RWKV_HEAD_DIM)
    head_ones = jnp.where(same_head, 1.0, 0.0).astype(BF16)

    def head_sum(x):
        xb = _bf(x)
        return jnp.concatenate(
            [_dot(xb[:, j * pair:(j + 1) * pair], head_ones) for j in range(n_pairs)], axis=1)

    kk = k * kk_ref[...]
    kmod = k * (1.0 + (a - 1.0) * ka_ref[...])
    sums = head_sum(jnp.concatenate([kk * kk, r * kmod * rk_ref[...]], axis=0))
    kk = kk / jnp.maximum(jnp.sqrt(sums[:c]), 1e-12)
    bonus_scale = sums[c:]
    bvec = kk * a

    tri_r = lax.broadcasted_iota(jnp.int32, (c, c), 0)
    tri_c = lax.broadcasted_iota(jnp.int32, (c, c), 1)
    tril = jnp.where(tri_c <= tri_r, 1.0, 0.0).astype(BF16)
    gcum = _chunk_cumsum(lw, tril)
    glast = gcum[c - 1:c, :]
    e_pos = jnp.exp(gcum)
    e_neg = jnp.exp(-gcum)
    e_tail = jnp.exp(glast - gcum)
    r_t = r * e_pos
    kap_t = kk * jnp.exp(gcum - lw)
    b_t = bvec * e_neg
    k_t = kmod * e_neg
    b_dec = bvec * e_tail
    k_dec = kmod * e_tail
    gamma_last = jnp.exp(glast)

    wt = lax.broadcasted_iota(jnp.int32, (c, pair), 0)
    wl = lax.broadcasted_iota(jnp.int32, (c, pair), 1)
    ws = wl & (RWKV_HEAD_DIM - 1)
    lane_lo = wl < RWKV_HEAD_DIM
    m_strict = ws < wt
    m_incl = ws <= wt
    m_16 = m_strict & ((ws >> 4) == (wt >> 4))
    m_32 = m_strict & ((ws >> 5) == (wt >> 5)) & ((ws >> 4) != (wt >> 4))
    m_64 = m_strict & ((ws >> 5) != (wt >> 5))
    eye_w = jnp.where(ws == wt, 1.0, 0.0).astype(F32)
    zero_b = jnp.zeros((c, pair), BF16)

    def stack2(x):
        xb = _bf(x)
        return jnp.concatenate([jnp.where(lane_lo, xb, zero_b), jnp.where(lane_lo, zero_b, xb)], axis=0)

    pairs = range(n_pairs)
    cols = [slice(j * pair, (j + 1) * pair) for j in pairs]
    lhs = [_bf(jnp.concatenate([kap_t[:, s], r_t[:, s]], axis=0)) for s in cols]
    bk_s = [jnp.concatenate([stack2(b_t[:, s]), stack2(k_t[:, s])], axis=0) for s in cols]
    v_s = [stack2(v[:, s]) for s in cols]
    scores = [_dot_nt(lhs[j], bk_s[j]) for j in pairs]
    st = [state_ref[j] for j in pairs]
    proj = [_dot_nt(lhs[j], _bf(st[j])) for j in pairs]
    a_w = [jnp.where(m_strict, scores[j][:c, :pair], 0.0) for j in pairs]
    b_w = [_bf(jnp.where(m_strict, scores[j][:c, pair:], 0.0)) for j in pairs]
    r_w = [_bf(jnp.concatenate([jnp.where(m_incl, scores[j][c:, :pair], 0.0),
                                jnp.where(m_incl, scores[j][c:, pair:], 0.0)], axis=1)) for j in pairs]
    rhs_u = [-(proj[j][:c] + _dot(b_w[j], v_s[j])) for j in pairs]

    d1 = [jnp.where(m_16, a_w[j], 0.0) for j in pairs]
    e32 = [stack2(jnp.where(m_32, a_w[j], 0.0)) for j in pairs]
    e64 = [_bf(jnp.where(m_64, a_w[j], 0.0)) for j in pairs]
    t_inv = [eye_w - d1[j] for j in pairs]
    d2 = [_dot(_bf(d1[j]), stack2(d1[j])) for j in pairs]
    nxt = [_dot(_bf(jnp.concatenate([t_inv[j], d2[j]], axis=0)), stack2(d2[j])) for j in pairs]
    t_inv = [t_inv[j] + nxt[j][:c] for j in pairs]
    d4 = [nxt[j][c:] for j in pairs]
    nxt = [_dot(_bf(jnp.concatenate([t_inv[j], d4[j]], axis=0)), stack2(d4[j])) for j in pairs]
    t_inv = [t_inv[j] + nxt[j][:c] for j in pairs]
    t_inv = [t_inv[j] + _dot(_bf(t_inv[j]), stack2(nxt[j][c:])) for j in pairs]
    te = [_dot(_bf(t_inv[j]), e32[j]) for j in pairs]
    t32 = [_bf(t_inv[j] - _dot(_bf(te[j]), stack2(t_inv[j]))) for j in pairs]
    z = [_dot(t32[j], stack2(rhs_u[j])) for j in pairs]
    ez = [_dot(e64[j], stack2(z[j])) for j in pairs]
    u = [z[j] - _dot(t32[j], stack2(ez[j])) for j in pairs]

    uv_s = [jnp.concatenate([stack2(u[j]), v_s[j]], axis=0) for j in pairs]
    y_cols = [proj[j][c:] + _dot(r_w[j], uv_s[j]) for j in pairs]

    rj = lax.broadcasted_iota(jnp.int32, (pair, pair), 0)
    cj = lax.broadcasted_iota(jnp.int32, (pair, pair), 1)
    same_head_sq = (rj < RWKV_HEAD_DIM) == (cj < RWKV_HEAD_DIM)
    for j in pairs:
        s = cols[j]
        uv = _bf(jnp.concatenate([u[j], v[:, s]], axis=0))
        bk = _bf(jnp.concatenate([b_dec[:, s], k_dec[:, s]], axis=0))
        ds = jnp.where(same_head_sq, _dot_tn(uv, bk), 0.0)
        state_ref[j] = st[j] * gamma_last[:, s] + ds

    y = jnp.concatenate(y_cols, axis=1)
    inv_n = 1.0 / RWKV_HEAD_DIM
    mean = head_sum(y) * inv_n
    yc = y - mean
    var = head_sum(yc * yc) * inv_n
    yn = yc * lax.rsqrt(var + RWKV_GN_EPS) * lnw_ref[...] + lnb_ref[...]
    y_ref[...] = ((yn + bonus_scale * v) * g).astype(y_ref.dtype)


def _lane_window(start, length):
    return (start // V7X_LANES * V7X_LANES, _round_up(start + length, V7X_LANES))


def _rows_in_window(w, window, start):
    top = start - window[0]
    return jnp.pad(w, ((top, window[1] - window[0] - top - w.shape[0]), (0, 0))).astype(BF16)


def _rwkv_mix(p, mu, w0, w_w2, a0, w_a2, w_g2, k_k, k_a, r_k, ln_w, ln_b, *, batch, seq, e):
    t, width = p.shape
    lw, la, lg = w_w2.shape[0], w_a2.shape[0], w_g2.shape[0]
    assert CHUNK == RWKV_HEAD_DIM and e % (2 * RWKV_HEAD_DIM) == 0 and seq % CHUNK == 0
    win_w = _lane_window(3 * e, lw)
    win_a = _lane_window(3 * e + lw, la)
    win_g = _lane_window(3 * e + lw + la, lg)
    ww2 = _rows_in_window(w_w2, win_w, 3 * e)
    wa2 = _rows_in_window(w_a2, win_a, 3 * e + lw)
    wg2 = _rows_in_window(w_g2, win_g, 3 * e + lw + la)
    mu_p = jnp.pad(mu, (0, width - mu.shape[0])).reshape(1, width)
    row = lambda a: a.reshape(1, -1)
    nc = seq // CHUNK
    n_pairs = e // (2 * RWKV_HEAD_DIM)
    vec = lambda n: pl.BlockSpec((1, n), lambda b, i: (0, 0))
    full = lambda a: pl.BlockSpec(a.shape, lambda b, i: (0, 0))
    blk = _nbytes((CHUNK, width), F32) + _nbytes((CHUNK, e), BF16) \
        + sum(_nbytes(a.shape, a.dtype) for a in (ww2, wa2, wg2)) + 24 * _nbytes((CHUNK, e), F32)
    return pl.pallas_call(
        functools.partial(_rwkv_kernel, e=e, win_w=win_w, win_a=win_a, win_g=win_g),
        out_shape=jax.ShapeDtypeStruct((t, e), BF16),
        grid=(batch, nc),
        in_specs=[pl.BlockSpec((CHUNK, width), lambda b, i: (b * nc + i, 0)),
                  vec(width), vec(e), full(ww2), vec(e), full(wa2), full(wg2),
                  vec(e), vec(e), vec(e), vec(e), vec(e)],
        out_specs=pl.BlockSpec((CHUNK, e), lambda b, i: (b * nc + i, 0)),
        scratch_shapes=[pltpu.VMEM((V7X_SUBLANES, width), F32),
                        pltpu.VMEM((n_pairs, 2 * RWKV_HEAD_DIM, 2 * RWKV_HEAD_DIM), F32)],
        compiler_params=_params(2, blk),
        name="rwkv7_mix",
    )(p, mu_p, row(w0), ww2, row(a0), wa2, wg2, row(k_k), row(k_a), row(r_k), row(ln_w), row(ln_b))


def _gla_kernel(p_ref, go_ref, wa2_ref, ba_ref, ng_ref, y_ref, state_ref, *, heads, dk, dv):
    c = p_ref.shape[0]
    kw = heads * dk
    e = heads * dv

    @pl.when(pl.program_id(1) == 0)
    def _():
        state_ref[...] = jnp.zeros_like(state_ref)

    p = p_ref[...]
    q = p[:, 0:kw]
    k = p[:, kw:2 * kw]
    v = p[:, 2 * kw:2 * kw + e]
    ad = p[:, 2 * kw + e:2 * kw + e + wa2_ref.shape[0]]
    go = go_ref[...]

    z = _dot(_bf(ad), wa2_ref[...]) + ba_ref[...]
    gk = (jnp.minimum(z, 0.0) - jnp.log(1.0 + jnp.exp(-jnp.abs(z)))) * (1.0 / GLA_GATE_NORM)

    tri_r = lax.broadcasted_iota(jnp.int32, (c, c), 0)
    tri_c = lax.broadcasted_iota(jnp.int32, (c, c), 1)
    causal = tri_c <= tri_r
    tril = jnp.where(causal, 1.0, 0.0).astype(BF16)
    bcum = _chunk_cumsum(gk, tril)
    blast = bcum[c - 1:c, :]
    q_dec = _bf(q * (dk ** -0.5) * jnp.exp(bcum))
    k_inv = _bf(k * jnp.exp(-bcum))
    k_dec = _bf(k * jnp.exp(blast - bcum))
    chunk_decay = jnp.exp(blast)

    outs = []
    for h in range(heads):
        ks = slice(h * dk, (h + 1) * dk)
        vh = _bf(v[:, h * dv:(h + 1) * dv])
        att = jnp.where(causal, _dot_nt(q_dec[:, ks], k_inv[:, ks]), 0.0)
        st = state_ref[h]
        oh = _dot(_bf(att), vh) + _dot_nt(q_dec[:, ks], _bf(st))
        state_ref[h] = st * chunk_decay[:, ks] + _dot_tn(vh, k_dec[:, ks])
        ms = jnp.mean(oh * oh, axis=-1, keepdims=True)
        outs.append(oh * lax.rsqrt(ms + GLA_NORM_EPS) * ng_ref[...])
    o = jnp.concatenate(outs, axis=1)
    y_ref[...] = (o * (go * _sigmoid(go))).astype(y_ref.dtype)


def _gla_mix(p, go, w_a2, b_a, norm_g, *, batch, seq, heads, dk, dv):
    t, width = p.shape
    e = heads * dv
    wa2 = jnp.pad(w_a2, ((0, V7X_LANES - w_a2.shape[0]), (0, 0))).astype(BF16)
    nc = seq // CHUNK
    blk = _nbytes((CHUNK, width), F32) + _nbytes((CHUNK, e), F32) + _nbytes((CHUNK, e), BF16) \
        + _nbytes(wa2.shape, wa2.dtype) + 16 * _nbytes((CHUNK, e), F32)
    return pl.pallas_call(
        functools.partial(_gla_kernel, heads=heads, dk=dk, dv=dv),
        out_shape=jax.ShapeDtypeStruct((t, e), BF16),
        grid=(batch, nc),
        in_specs=[pl.BlockSpec((CHUNK, width), lambda b, i: (b * nc + i, 0)),
                  pl.BlockSpec((CHUNK, e), lambda b, i: (b * nc + i, 0)),
                  pl.BlockSpec(wa2.shape, lambda b, i: (0, 0)),
                  pl.BlockSpec((1, heads * dk), lambda b, i: (0, 0)),
                  pl.BlockSpec((1, dv), lambda b, i: (0, 0))],
        out_specs=pl.BlockSpec((CHUNK, e), lambda b, i: (b * nc + i, 0)),
        scratch_shapes=[pltpu.VMEM((heads, dv, dk), F32)],
        compiler_params=_params(2, blk),
        name="gla_mix",
    )(p, go, wa2, b_a.reshape(1, -1), norm_g.reshape(1, -1))


def _causal_conv3(u, prev8, w):
    return _shift_rows(u, prev8, 2) * w[0:1] + _shift_rows(u, prev8, 1) * w[1:2] + u * w[2:3]


def _sconv_kernel(bg_ref, cg_ref, hh_ref, w_ref, y_ref, carry_ref):
    @pl.when(pl.program_id(1) == 0)
    def _():
        carry_ref[...] = jnp.zeros_like(carry_ref)

    u = cg_ref[...] * hh_ref[...]
    y = bg_ref[...] * _causal_conv3(u, carry_ref[...], w_ref[...])
    carry_ref[...] = u[u.shape[0] - V7X_SUBLANES:]
    y_ref[...] = y.astype(y_ref.dtype)


def _sconv_mix(p, conv_w, *, batch, seq, e):
    t = p.shape[0]
    tm = _pick_tile(seq, 512, V7X_SUBLANES)
    ms = seq // tm
    col = lambda g: pl.BlockSpec((tm, e), lambda b, i, g=g: (b * ms + i, g))
    blk = 3 * _nbytes((tm, e), F32) + _nbytes((tm, e), BF16) + 4 * _nbytes((tm, e), F32)
    return pl.pallas_call(
        _sconv_kernel,
        out_shape=jax.ShapeDtypeStruct((t, e), BF16),
        grid=(batch, ms),
        in_specs=[col(0), col(1), col(2), pl.BlockSpec((CONV_WIDTH, e), lambda b, i: (0, 0))],
        out_specs=pl.BlockSpec((tm, e), lambda b, i: (b * ms + i, 0)),
        scratch_shapes=[pltpu.VMEM((V7X_SUBLANES, e), F32)],
        compiler_params=_params(2, blk),
        name="short_conv_mix",
    )(p, p, p, conv_w)


def _merge_kernel(ya_ref, yb_ref, yc_ref, wa_ref, wb_ref, wc_ref, ga_ref, gb_ref, gc_ref, o_ref, wbf_ref):
    @pl.when(pl.program_id(1) == 0)
    def _():
        for g, w_ref in enumerate((wa_ref, wb_ref, wc_ref)):
            _cast_rows(w_ref, wbf_ref.at[g])

    acc = ga_ref[...].astype(F32) * _dot(ya_ref[...], wbf_ref[0])
    acc = acc + gb_ref[...].astype(F32) * _dot(yb_ref[...], wbf_ref[1])
    acc = acc + gc_ref[...].astype(F32) * _dot(yc_ref[...], wbf_ref[2])
    o_ref[...] = acc.astype(o_ref.dtype)


def _merge(ya, yb, yc, w_br, layer, gates):
    t, e = ya.shape
    d = w_br.shape[-1]
    tm = _pick_tile(t, 512, V7X_SUBLANES)
    tn = _pick_tile(d, 512)
    nt = d // tn
    ysp = pl.BlockSpec((tm, e), lambda j, i: (i, 0))
    wsp = lambda g: pl.BlockSpec((None, None, e, tn), lambda j, i, g=g: (layer, g, 0, j))
    gsp = lambda g: pl.BlockSpec((tm, tn), lambda j, i, g=g: (i, g * nt + j))
    blk = 3 * _nbytes((tm, e), BF16) + 3 * _nbytes((e, tn), F32) + 3 * _nbytes((tm, tn), gates.dtype) \
        + _nbytes((tm, tn), BF16) + 2 * _nbytes((tm, tn), F32) + 3 * _nbytes((e, tn), BF16) // 2
    return pl.pallas_call(
        _merge_kernel,
        out_shape=jax.ShapeDtypeStruct((t, d), BF16),
        grid=(nt, t // tm),
        in_specs=[ysp, ysp, ysp, wsp(0), wsp(1), wsp(2), gsp(0), gsp(1), gsp(2)],
        out_specs=pl.BlockSpec((tm, tn), lambda j, i: (i, j)),
        scratch_shapes=[pltpu.VMEM((3, e, tn), BF16)],
        compiler_params=_params(2, blk),
        name="gated_merge",
    )(ya, yb, yc, w_br, w_br, w_br, gates, gates, gates)


FFN_PIPE_CHUNKS = 16


def _exact_zero_like_row(x):
    r, n = x.shape
    top = jnp.max(x.reshape(r // V7X_SUBLANES, V7X_SUBLANES, n), axis=0)
    bits = pltpu.bitcast(top, jnp.uint32)
    z = lax.shift_right_logical(lax.shift_right_logical(bits, jnp.uint32(16)), jnp.uint32(16))
    return pltpu.bitcast(z, F32)[0:1].astype(BF16)


def _ffn_up_kernel(h_ref, wg_ref, wv_ref, cg_ref, cv_ref, act_ref, wbf_ref, stage_ref, carry_ref, *, ms, bm):
    j = pl.program_id(0)
    slot = lax.rem(j, 2)

    @pl.when(j == 0)
    def _():
        stage_ref[...] = jnp.zeros_like(stage_ref)
        carry_ref[...] = jnp.zeros_like(carry_ref)

    @pl.when(lax.rem(j, bm) == 0)
    def _():
        _cast_rows(wg_ref, wbf_ref.at[0])
        _cast_rows(wv_ref, wbf_ref.at[1])

    jp = jnp.maximum(j - 1, 0)
    first_rows = lax.rem(jp, ms) == 0
    prev_g = jnp.where(first_rows, 0.0, carry_ref[0])
    prev_v = jnp.where(first_rows, 0.0, carry_ref[1])

    tm, d = h_ref.shape
    rows = tm // FFN_PIPE_CHUNKS
    kc = d // FFN_PIPE_CHUNKS
    acc_g = None
    acc_v = None
    for c in range(FFN_PIPE_CHUNKS):
        rs = slice(c * rows, (c + 1) * rows)
        ug = stage_ref[1 - slot, 0, rs, :]
        uv = stage_ref[1 - slot, 1, rs, :]
        xg = _causal_conv3(ug, prev_g, cg_ref[...])
        xv = _causal_conv3(uv, prev_v, cv_ref[...])
        gelu = 0.5 * xg * (1.0 + jnp.tanh(0.7978845608028654 * (xg + 0.044715 * (xg * xg * xg))))
        act = gelu * xv
        act_ref[rs, :] = act.astype(act_ref.dtype)
        prev_g = ug[rows - V7X_SUBLANES:]
        prev_v = uv[rows - V7X_SUBLANES:]

        zero = _exact_zero_like_row(act)
        ks = slice(c * kc, (c + 1) * kc)
        hk = h_ref[:, ks]
        pg = _dot(hk, wbf_ref[0, ks, :] + zero)
        pv = _dot(hk, wbf_ref[1, ks, :] + zero)
        acc_g = pg if acc_g is None else acc_g + pg
        acc_v = pv if acc_v is None else acc_v + pv
    carry_ref[0] = prev_g
    carry_ref[1] = prev_v
    stage_ref[slot, 0] = acc_g
    stage_ref[slot, 1] = acc_v


def _ffn_up(h, w_up, layer, conv_w, *, batch, seq):
    t, d = h.shape
    f = w_up.shape[-1] // 2
    tm = _pick_tile(seq, 1024, V7X_SUBLANES)
    tn = _pick_tile(f, 256)
    ms, nt = seq // tm, f // tn
    bm = batch * ms
    last = nt * bm - 1
    cur = lambda j: jnp.minimum(j, last)
    prv = lambda j: jnp.maximum(j - 1, 0)
    assert tm % (FFN_PIPE_CHUNKS * V7X_SUBLANES) == 0 and d % (FFN_PIPE_CHUNKS * V7X_LANES) == 0
    blk = _nbytes((tm, d), BF16) + 2 * _nbytes((d, tn), F32) + _nbytes((tm, tn), BF16) \
        + 4 * _nbytes((tm, tn), F32) + _nbytes((d, tn), BF16)
    return pl.pallas_call(
        functools.partial(_ffn_up_kernel, ms=ms, bm=bm),
        out_shape=jax.ShapeDtypeStruct((t, f), BF16),
        grid=(nt * bm + 1,),
        in_specs=[pl.BlockSpec((tm, d), lambda j: (lax.rem(cur(j), bm), 0)),
                  pl.BlockSpec((None, d, tn), lambda j: (layer, 0, lax.div(cur(j), bm))),
                  pl.BlockSpec((None, d, tn), lambda j: (layer, 0, nt + lax.div(cur(j), bm))),
                  pl.BlockSpec((CONV_WIDTH, tn), lambda j: (0, lax.div(prv(j), bm))),
                  pl.BlockSpec((CONV_WIDTH, tn), lambda j: (0, nt + lax.div(prv(j), bm)))],
        out_specs=pl.BlockSpec((tm, tn), lambda j: (lax.rem(prv(j), bm), lax.div(prv(j), bm))),
        scratch_shapes=[pltpu.VMEM((2, d, tn), BF16),
                        pltpu.VMEM((2, 2, tm, tn), F32),
                        pltpu.VMEM((2, V7X_SUBLANES, tn), F32)],
        compiler_params=_params(1, blk),
        name="ffn_up_conv_gate",
    )(h, w_up, w_up, conv_w, conv_w)


def _layer(x, h, l, prm, next_pre_g, *, batch, seq):
    (w_in, rwkv_mu, rwkv_w0, rwkv_w_w2, rwkv_a0, rwkv_w_a2, rwkv_w_g2, rwkv_k_k, rwkv_k_a, rwkv_r_k,
     rwkv_ln_w, rwkv_ln_b, gla_w_a2, gla_b_a, gla_norm_g, sconv_w, w_br, w_o, post_mix_g, pre_ffn_g,
     w_up, ffn_conv_w, w_down, post_ffn_g) = prm
    d = x.shape[1]
    e = w_br.shape[2]
    lora = rwkv_w_w2.shape[1] + rwkv_w_a2.shape[1] + rwkv_w_g2.shape[1]
    gr, kw = gla_w_a2.shape[1:]
    dv = gla_norm_g.shape[1]
    heads = e // dv
    dk = kw // heads

    o_gla = 3 * e + lora
    o_go = o_gla + 2 * kw + e + gr
    o_conv = o_go + e
    o_gate = o_conv + 3 * e

    def in_proj(start, need, name, **kw_):
        width, tn = _tile_columns(need)
        return _proj(h, w_in, l, start, width, w_is_nk=True, tn=tn, name=name, **kw_)

    p_rwkv = in_proj(0, o_gla, "proj_rwkv", out_dtype=F32)
    p_gla = in_proj(o_gla, 2 * kw + e + V7X_LANES, "proj_gla", out_dtype=F32)
    p_go = in_proj(o_go, e, "proj_gla_gate", out_dtype=F32)
    p_conv = in_proj(o_conv, 3 * e, "proj_conv", out_dtype=F32)
    gates = in_proj(o_gate, 3 * d, "proj_gate", out_dtype=BF16, act="sigmoid")

    y_a = _rwkv_mix(p_rwkv, rwkv_mu[l], rwkv_w0[l], rwkv_w_w2[l], rwkv_a0[l], rwkv_w_a2[l], rwkv_w_g2[l],
                    rwkv_k_k[l], rwkv_k_a[l], rwkv_r_k[l], rwkv_ln_w[l], rwkv_ln_b[l],
                    batch=batch, seq=seq, e=e)
    y_b = _gla_mix(p_gla, p_go, gla_w_a2[l], gla_b_a[l], gla_norm_g[l],
                   batch=batch, seq=seq, heads=heads, dk=dk, dv=dv)
    y_c = _sconv_mix(p_conv, sconv_w[l], batch=batch, seq=seq, e=e)

    merged = _merge(y_a, y_b, y_c, w_br, l, gates)
    o = _proj(merged, w_o, l, 0, d, w_is_nk=False, tn=_tile_columns(d)[1], out_dtype=BF16, name="proj_out")
    x, h2 = _residual_norm(o, x, post_mix_g[l], pre_ffn_g[l])

    act = _ffn_up(h2, w_up, l, ffn_conv_w[l], batch=batch, seq=seq)
    o2 = _matmul(act, w_down[l].astype(BF16), out_dtype=BF16, tm=512, tn=512, name="ffn_down")
    return _residual_norm(o2, x, post_ffn_g[l], next_pre_g)


def kernel(x, pre_mix_g, w_in, rwkv_mu, rwkv_w0, rwkv_w_w2, rwkv_a0, rwkv_w_a2, rwkv_w_g2, rwkv_k_k, rwkv_k_a, rwkv_r_k, rwkv_ln_w, rwkv_ln_b, gla_w_a2, gla_b_a, gla_norm_g, sconv_w, w_br, w_o, post_mix_g, pre_ffn_g, w_up, ffn_conv_w, w_down, post_ffn_g):
    batch, seq, d = x.shape
    depth = w_in.shape[0]
    prm = (jnp.swapaxes(w_in, 1, 2), rwkv_mu, rwkv_w0, rwkv_w_w2, rwkv_a0, rwkv_w_a2, rwkv_w_g2, rwkv_k_k, rwkv_k_a,
           rwkv_r_k.reshape(depth, -1), rwkv_ln_w, rwkv_ln_b, gla_w_a2, gla_b_a, gla_norm_g, sconv_w,
           w_br, w_o, post_mix_g, pre_ffn_g, w_up, ffn_conv_w, w_down, post_ffn_g)
    xf = x.reshape(batch * seq, d)
    h = _rms_norm_bf16(xf, pre_mix_g[0])
    for l in range(depth):
        next_g = pre_mix_g[l + 1] if l + 1 < depth else None
        xf, h = _layer(xf, h, l, prm, next_g, batch=batch, seq=seq)
    return xf.reshape(batch, seq, d)
```

```python
import functools

import jax
import jax.numpy as jnp
from jax import lax
from jax.experimental import pallas as pl
from jax.experimental.pallas import tpu as pltpu

F32 = jnp.float32
BF16 = jnp.bfloat16

NORM_EPS = 1e-6
RWKV_HEAD_DIM = 64
RWKV_GN_EPS = 1e-5 * RWKV_HEAD_DIM
GLA_GATE_NORM = 16.0
GLA_NORM_EPS = 1e-5
CONV_WIDTH = 3
CHUNK = 64

V7X_LANES = 128
V7X_SUBLANES = 8
V7X_VMEM_BYTES = 64 * 1024 * 1024
VMEM_CAP_BYTES = V7X_VMEM_BYTES - 6 * 1024 * 1024

CAST_VREGS = 256


def _round_up(n, m):
    return (n + m - 1) // m * m


def _pick_tile(n, pref, quantum=V7X_LANES):
    if n <= pref:
        return n
    t = pref - pref % quantum
    while t >= quantum:
        if n % t == 0:
            return t
        t -= quantum
    return n


def _params(n_axes, block_bytes):
    limit = min(VMEM_CAP_BYTES, 2 * block_bytes + 16 * 1024 * 1024)
    return pltpu.CompilerParams(
        dimension_semantics=("arbitrary",) * n_axes, vmem_limit_bytes=int(limit))


def _nbytes(shape, dtype):
    n = 1
    for s in shape:
        n *= s
    return n * jnp.dtype(dtype).itemsize


def _dot(a, b):
    return jnp.dot(a, b, preferred_element_type=F32)


def _dot_nt(a, b):
    return lax.dot_general(a, b, (((1,), (1,)), ((), ())), preferred_element_type=F32)


def _dot_tn(a, b):
    return lax.dot_general(a, b, (((0,), (0,)), ((), ())), preferred_element_type=F32)


def _bf(x):
    return x.astype(BF16)


def _sigmoid(x):
    return 1.0 / (1.0 + jnp.exp(-x))


def _chunk_cumsum(x, tril_bf):
    hi = _bf(x)
    lo = _bf(x - hi.astype(F32))
    return _dot(tril_bf, hi) + _dot(tril_bf, lo)


def _shift_rows(u, prev8, k):
    r, n = u.shape
    g = r // V7X_SUBLANES
    rolled = pltpu.roll(u.reshape(g, V7X_SUBLANES, n), k, 1)
    above = jnp.concatenate([pltpu.roll(prev8, k, 0)[None], rolled[:g - 1]], axis=0)
    sub = lax.broadcasted_iota(jnp.int32, (g, V7X_SUBLANES, n), 1)
    return jnp.where(sub < k, above, rolled).reshape(r, n)


def _cast_rows(src_ref, dst_ref):
    rows, width = src_ref.shape
    step = _pick_tile(rows, max(CAST_VREGS * V7X_SUBLANES * V7X_LANES // width, 2 * V7X_SUBLANES),
                      2 * V7X_SUBLANES)
    for r in range(0, rows, step):
        dst_ref[r:r + step, :] = _bf(src_ref[r:r + step, :])


def _rms_kernel(x_ref, g_ref, h_ref):
    x = x_ref[...]
    ms = jnp.mean(x * x, axis=-1, keepdims=True)
    h_ref[...] = (x * lax.rsqrt(ms + NORM_EPS) * g_ref[...]).astype(h_ref.dtype)


def _rms_norm_bf16(x, g):
    t, d = x.shape
    tm = _pick_tile(t, 256, V7X_SUBLANES)
    blk = _nbytes((tm, d), F32) + _nbytes((tm, d), BF16)
    return pl.pallas_call(
        _rms_kernel,
        out_shape=jax.ShapeDtypeStruct((t, d), BF16),
        grid=(t // tm,),
        in_specs=[pl.BlockSpec((tm, d), lambda i: (i, 0)),
                  pl.BlockSpec((1, d), lambda i: (0, 0))],
        out_specs=pl.BlockSpec((tm, d), lambda i: (i, 0)),
        compiler_params=_params(1, blk),
        name="rms_norm",
    )(x, g.reshape(1, d))


def _residual_kernel(o_ref, x_ref, gpost_ref, *rest, with_next):
    o = o_ref[...].astype(F32)
    ms = jnp.mean(o * o, axis=-1, keepdims=True)
    xn = x_ref[...] + o * lax.rsqrt(ms + NORM_EPS) * gpost_ref[...]
    if with_next:
        gpre_ref, xo_ref, h_ref = rest
        ms2 = jnp.mean(xn * xn, axis=-1, keepdims=True)
        h_ref[...] = (xn * lax.rsqrt(ms2 + NORM_EPS) * gpre_ref[...]).astype(h_ref.dtype)
    else:
        (xo_ref,) = rest
    xo_ref[...] = xn


def _residual_norm(o, x, g_post, g_next):
    t, d = x.shape
    tm = _pick_tile(t, 256, V7X_SUBLANES)
    with_next = g_next is not None
    row = pl.BlockSpec((tm, d), lambda i: (i, 0))
    vec = pl.BlockSpec((1, d), lambda i: (0, 0))
    in_specs = [row, row, vec] + ([vec] if with_next else [])
    args = [o, x, g_post.reshape(1, d)] + ([g_next.reshape(1, d)] if with_next else [])
    out_shape = [jax.ShapeDtypeStruct((t, d), F32)]
    out_specs = [row]
    if with_next:
        out_shape.append(jax.ShapeDtypeStruct((t, d), BF16))
        out_specs.append(row)
    blk = 3 * _nbytes((tm, d), F32) + _nbytes((tm, d), BF16)
    res = pl.pallas_call(
        functools.partial(_residual_kernel, with_next=with_next),
        out_shape=out_shape,
        grid=(t // tm,),
        in_specs=in_specs,
        out_specs=out_specs,
        compiler_params=_params(1, blk),
        name="residual_norm",
    )(*args)
    return (res[0], res[1]) if with_next else (res[0], None)


def _mm_kernel(x_ref, w_ref, o_ref):
    o_ref[...] = _dot(x_ref[...], w_ref[...]).astype(o_ref.dtype)


def _matmul(x, w, layer, *, out_dtype, tm, tn, name):
    m, k = x.shape
    n = w.shape[2]
    tm = _pick_tile(m, tm, V7X_SUBLANES)
    tn = _pick_tile(n, tn)
    blk = _nbytes((tm, k), x.dtype) + _nbytes((k, tn), w.dtype) + _nbytes((tm, tn), out_dtype) \
        + _nbytes((tm, tn), F32)
    return pl.pallas_call(
        _mm_kernel,
        out_shape=jax.ShapeDtypeStruct((m, n), out_dtype),
        grid=(m // tm, n // tn),
        in_specs=[pl.BlockSpec((tm, k), lambda i, j: (i, 0)),
                  pl.BlockSpec((None, k, tn), lambda i, j: (layer, 0, j))],
        out_specs=pl.BlockSpec((tm, tn), lambda i, j: (i, j)),
        compiler_params=_params(2, blk),
        name=name,
    )(x, w)


PROJ_TN_CHOICES = (1024, 768, 512)


def _tile_columns(need):
    return min(((_round_up(need, tn), tn) for tn in PROJ_TN_CHOICES), key=lambda wt: (wt[0], -wt[1]))


def _proj_kernel(x_ref, w_ref, o_ref, wbf_ref, *, w_is_nk, act):
    @pl.when(pl.program_id(1) == 0)
    def _():
        _cast_rows(w_ref, wbf_ref)

    acc = _dot_nt(x_ref[...], wbf_ref[...]) if w_is_nk else _dot(x_ref[...], wbf_ref[...])
    if act == "sigmoid":
        acc = _sigmoid(acc)
    o_ref[...] = acc.astype(o_ref.dtype)


def _proj(x, w, layer, start, width, *, w_is_nk, tn, out_dtype, act=None, name):
    m, k = x.shape
    assert width % tn == 0
    out_bytes = _nbytes((1, tn), out_dtype)
    w_bytes = _nbytes((k, tn), F32)

    def fits(t):
        need = 2 * (_nbytes((t, k), x.dtype) + w_bytes + t * out_bytes) + w_bytes // 2 + _nbytes((t, tn), F32)
        return need <= VMEM_CAP_BYTES - 4 * 1024 * 1024

    tm = next((t for t in (1024, 512, 256, 128) if m % t == 0 and fits(t)), m)
    if w_is_nk:
        assert start % V7X_SUBLANES == 0 and start + width <= w.shape[1]
        wspec = pl.BlockSpec((pl.Squeezed(), pl.Element(tn), pl.Element(k)),
                             lambda j, i: (layer, pl.multiple_of(start + j * tn, V7X_SUBLANES), 0))
        wshape = (tn, k)
    else:
        assert start % tn == 0 and start + width <= w.shape[2]
        wspec = pl.BlockSpec((None, k, tn), lambda j, i: (layer, 0, start // tn + j))
        wshape = (k, tn)
    blk = _nbytes((tm, k), x.dtype) + w_bytes + tm * out_bytes + (_nbytes((tm, tn), F32) + w_bytes // 2) // 2
    return pl.pallas_call(
        functools.partial(_proj_kernel, w_is_nk=w_is_nk, act=act),
        out_shape=jax.ShapeDtypeStruct((m, width), out_dtype),
        grid=(width // tn, m // tm),
        in_specs=[pl.BlockSpec((tm, k), lambda j, i: (i, 0)), wspec],
        out_specs=pl.BlockSpec((tm, tn), lambda j, i: (i, j)),
        scratch_shapes=[pltpu.VMEM(wshape, BF16)],
        compiler_params=_params(2, blk),
        name=name,
    )(x, w)


def _rwkv_kernel(p_ref, mu_ref, w0_ref, ww2_ref, a0_ref, wa2_ref, wg2_ref, kk_ref, ka_ref, rk_ref,
                 lnw_ref, lnb_ref, y_ref, prev_ref, state_ref, *, e, win_w, win_a, win_g):
    c = p_ref.shape[0]
    pair = 2 * RWKV_HEAD_DIM
    n_pairs = e // pair

    @pl.when(pl.program_id(1) == 0)
    def _():
        prev_ref[...] = jnp.zeros_like(prev_ref)
        state_ref[...] = jnp.zeros_like(state_ref)

    p = p_ref[...]
    shifted = _shift_rows(p, prev_ref[...], 1)
    prev_ref[...] = p[c - V7X_SUBLANES:]
    xm = p + (shifted - p) * mu_ref[...]

    r = xm[:, 0:e]
    k = xm[:, e:2 * e]
    v = xm[:, 2 * e:3 * e]
    wd = xm[:, win_w[0]:win_w[1]]
    ad = xm[:, win_a[0]:win_a[1]]
    gd = xm[:, win_g[0]:win_g[1]]

    w_pre = w0_ref[...] + _dot(_bf(jnp.tanh(wd)), ww2_ref[...])
    lw = (-0.6065306597126334) * _sigmoid(w_pre)
    a = _sigmoid(a0_ref[...] + _dot(_bf(ad), wa2_ref[...]))
    g = _dot(_bf(_sigmoid(gd)), wg2_ref[...])

    ri = lax.broadcasted_iota(jnp.int32, (pair, pair), 0)
    ci = lax.broadcasted_iota(jnp.int32, (pair, pair), 1)
    same_head = (ri < RWKV_HEAD_DIM) == (ci < RWKV_HEAD_DIM)
    head_ones = jnp.where(same_head, 1.0, 0.0).astype(BF16)

    def head_sum(x):
        xb = _bf(x)
        return jnp.concatenate(
            [_dot(xb[:, j * pair:(j + 1) * pair], head_ones) for j in range(n_pairs)], axis=1)

    kk = k * kk_ref[...]
    kmod = k * (1.0 + (a - 1.0) * ka_ref[...])
    sums = head_sum(jnp.concatenate([kk * kk, r * kmod * rk_ref[...]], axis=0))
    kk = kk / jnp.maximum(jnp.sqrt(sums[:c]), 1e-12)
    bonus_scale = sums[c:]
    bvec = kk * a

    tri_r = lax.broadcasted_iota(jnp.int32, (c, c), 0)
    tri_c = lax.broadcasted_iota(jnp.int32, (c, c), 1)
    tril = jnp.where(tri_c <= tri_r, 1.0, 0.0).astype(BF16)
    gcum = _chunk_cumsum(lw, tril)
    glast = gcum[c - 1:c, :]
    e_pos = jnp.exp(gcum)
    e_neg = jnp.exp(-gcum)
    e_tail = jnp.exp(glast - gcum)
    r_t = r * e_pos
    kap_t = kk * jnp.exp(gcum - lw)
    b_t = bvec * e_neg
    k_t = kmod * e_neg
    b_dec = bvec * e_tail
    k_dec = kmod * e_tail
    gamma_last = jnp.exp(glast)

    wt = lax.broadcasted_iota(jnp.int32, (c, pair), 0)
    wl = lax.broadcasted_iota(jnp.int32, (c, pair), 1)
    ws = wl & (RWKV_HEAD_DIM - 1)
    lane_lo = wl < RWKV_HEAD_DIM
    m_strict = ws < wt
    m_incl = ws <= wt
    m_16 = m_strict & ((ws >> 4) == (wt >> 4))
    m_32 = m_strict & ((ws >> 5) == (wt >> 5)) & ((ws >> 4) != (wt >> 4))
    m_64 = m_strict & ((ws >> 5) != (wt >> 5))
    eye_w = jnp.where(ws == wt, 1.0, 0.0).astype(F32)
    zero_b = jnp.zeros((c, pair), BF16)

    def stack2(x):
        xb = _bf(x)
        return jnp.concatenate([jnp.where(lane_lo, xb, zero_b), jnp.where(lane_lo, zero_b, xb)], axis=0)

    pairs = range(n_pairs)
    cols = [slice(j * pair, (j + 1) * pair) for j in pairs]
    lhs = [_bf(jnp.concatenate([kap_t[:, s], r_t[:, s]], axis=0)) for s in cols]
    bk_s = [jnp.concatenate([stack2(b_t[:, s]), stack2(k_t[:, s])], axis=0) for s in cols]
    v_s = [stack2(v[:, s]) for s in cols]
    scores = [_dot_nt(lhs[j], bk_s[j]) for j in pairs]
    st = [state_ref[j] for j in pairs]
    proj = [_dot_nt(lhs[j], _bf(st[j])) for j in pairs]
    a_w = [jnp.where(m_strict, scores[j][:c, :pair], 0.0) for j in pairs]
    b_w = [_bf(jnp.where(m_strict, scores[j][:c, pair:], 0.0)) for j in pairs]
    r_w = [_bf(jnp.concatenate([jnp.where(m_incl, scores[j][c:, :pair], 0.0),
                                jnp.where(m_incl, scores[j][c:, pair:], 0.0)], axis=1)) for j in pairs]
    rhs_u = [-(proj[j][:c] + _dot(b_w[j], v_s[j])) for j in pairs]

    d1 = [jnp.where(m_16, a_w[j], 0.0) for j in pairs]
    e32 = [stack2(jnp.where(m_32, a_w[j], 0.0)) for j in pairs]
    e64 = [_bf(jnp.where(m_64, a_w[j], 0.0)) for j in pairs]
    t_inv = [eye_w - d1[j] for j in pairs]
    d2 = [_dot(_bf(d1[j]), stack2(d1[j])) for j in pairs]
    nxt = [_dot(_bf(jnp.concatenate([t_inv[j], d2[j]], axis=0)), stack2(d2[j])) for j in pairs]
    t_inv = [t_inv[j] + nxt[j][:c] for j in pairs]
    d4 = [nxt[j][c:] for j in pairs]
    nxt = [_dot(_bf(jnp.concatenate([t_inv[j], d4[j]], axis=0)), stack2(d4[j])) for j in pairs]
    t_inv = [t_inv[j] + nxt[j][:c] for j in pairs]
    t_inv = [t_inv[j] + _dot(_bf(t_inv[j]), stack2(nxt[j][c:])) for j in pairs]
    te = [_dot(_bf(t_inv[j]), e32[j]) for j in pairs]
    t32 = [_bf(t_inv[j] - _dot(_bf(te[j]), stack2(t_inv[j]))) for j in pairs]
    z = [_dot(t32[j], stack2(rhs_u[j])) for j in pairs]
    ez = [_dot(e64[j], stack2(z[j])) for j in pairs]
    u = [z[j] - _dot(t32[j], stack2(ez[j])) for j in pairs]

    uv_s = [jnp.concatenate([stack2(u[j]), v_s[j]], axis=0) for j in pairs]
    y_cols = [proj[j][c:] + _dot(r_w[j], uv_s[j]) for j in pairs]

    rj = lax.broadcasted_iota(jnp.int32, (pair, pair), 0)
    cj = lax.broadcasted_iota(jnp.int32, (pair, pair), 1)
    same_head_sq = (rj < RWKV_HEAD_DIM) == (cj < RWKV_HEAD_DIM)
    for j in pairs:
        s = cols[j]
        uv = _bf(jnp.concatenate([u[j], v[:, s]], axis=0))
        bk = _bf(jnp.concatenate([b_dec[:, s], k_dec[:, s]], axis=0))
        ds = jnp.where(same_head_sq, _dot_tn(uv, bk), 0.0)
        state_ref[j] = st[j] * gamma_last[:, s] + ds

    y = jnp.concatenate(y_cols, axis=1)
    inv_n = 1.0 / RWKV_HEAD_DIM
    mean = head_sum(y) * inv_n
    yc = y - mean
    var = head_sum(yc * yc) * inv_n
    yn = yc * lax.rsqrt(var + RWKV_GN_EPS) * lnw_ref[...] + lnb_ref[...]
    y_ref[...] = ((yn + bonus_scale * v) * g).astype(y_ref.dtype)


def _lane_window(start, length):
    return (start // V7X_LANES * V7X_LANES, _round_up(start + length, V7X_LANES))


def _rows_in_window(w, window, start):
    top = start - window[0]
    return jnp.pad(w, ((top, window[1] - window[0] - top - w.shape[0]), (0, 0))).astype(BF16)


def _rwkv_mix(p, mu, w0, w_w2, a0, w_a2, w_g2, k_k, k_a, r_k, ln_w, ln_b, *, batch, seq, e):
    t, width = p.shape
    lw, la, lg = w_w2.shape[0], w_a2.shape[0], w_g2.shape[0]
    assert CHUNK == RWKV_HEAD_DIM and e % (2 * RWKV_HEAD_DIM) == 0 and seq % CHUNK == 0
    win_w = _lane_window(3 * e, lw)
    win_a = _lane_window(3 * e + lw, la)
    win_g = _lane_window(3 * e + lw + la, lg)
    ww2 = _rows_in_window(w_w2, win_w, 3 * e)
    wa2 = _rows_in_window(w_a2, win_a, 3 * e + lw)
    wg2 = _rows_in_window(w_g2, win_g, 3 * e + lw + la)
    mu_p = jnp.pad(mu, (0, width - mu.shape[0])).reshape(1, width)
    row = lambda a: a.reshape(1, -1)
    nc = seq // CHUNK
    n_pairs = e // (2 * RWKV_HEAD_DIM)
    vec = lambda n: pl.BlockSpec((1, n), lambda b, i: (0, 0))
    full = lambda a: pl.BlockSpec(a.shape, lambda b, i: (0, 0))
    blk = _nbytes((CHUNK, width), F32) + _nbytes((CHUNK, e), BF16) \
        + sum(_nbytes(a.shape, a.dtype) for a in (ww2, wa2, wg2)) + 24 * _nbytes((CHUNK, e), F32)
    return pl.pallas_call(
        functools.partial(_rwkv_kernel, e=e, win_w=win_w, win_a=win_a, win_g=win_g),
        out_shape=jax.ShapeDtypeStruct((t, e), BF16),
        grid=(batch, nc),
        in_specs=[pl.BlockSpec((CHUNK, width), lambda b, i: (b * nc + i, 0)),
                  vec(width), vec(e), full(ww2), vec(e), full(wa2), full(wg2),
                  vec(e), vec(e), vec(e), vec(e), vec(e)],
        out_specs=pl.BlockSpec((CHUNK, e), lambda b, i: (b * nc + i, 0)),
        scratch_shapes=[pltpu.VMEM((V7X_SUBLANES, width), F32),
                        pltpu.VMEM((n_pairs, 2 * RWKV_HEAD_DIM, 2 * RWKV_HEAD_DIM), F32)],
        compiler_params=_params(2, blk),
        name="rwkv7_mix",
    )(p, mu_p, row(w0), ww2, row(a0), wa2, wg2, row(k_k), row(k_a), row(r_k), row(ln_w), row(ln_b))


def _gla_kernel(p_ref, go_ref, wa2_ref, ba_ref, ng_ref, y_ref, state_ref, *, heads, dk, dv):
    c = p_ref.shape[0]
    kw = heads * dk
    e = heads * dv

    @pl.when(pl.program_id(1) == 0)
    def _():
        state_ref[...] = jnp.zeros_like(state_ref)

    p = p_ref[...]
    q = p[:, 0:kw]
    k = p[:, kw:2 * kw]
    v = p[:, 2 * kw:2 * kw + e]
    ad = p[:, 2 * kw + e:2 * kw + e + wa2_ref.shape[0]]
    go = go_ref[...]

    z = _dot(_bf(ad), wa2_ref[...]) + ba_ref[...]
    gk = (jnp.minimum(z, 0.0) - jnp.log(1.0 + jnp.exp(-jnp.abs(z)))) * (1.0 / GLA_GATE_NORM)

    tri_r = lax.broadcasted_iota(jnp.int32, (c, c), 0)
    tri_c = lax.broadcasted_iota(jnp.int32, (c, c), 1)
    causal = tri_c <= tri_r
    tril = jnp.where(causal, 1.0, 0.0).astype(BF16)
    bcum = _chunk_cumsum(gk, tril)
    blast = bcum[c - 1:c, :]
    q_dec = _bf(q * (dk ** -0.5) * jnp.exp(bcum))
    k_inv = _bf(k * jnp.exp(-bcum))
    k_dec = _bf(k * jnp.exp(blast - bcum))
    chunk_decay = jnp.exp(blast)

    outs = []
    for h in range(heads):
        ks = slice(h * dk, (h + 1) * dk)
        vh = _bf(v[:, h * dv:(h + 1) * dv])
        att = jnp.where(causal, _dot_nt(q_dec[:, ks], k_inv[:, ks]), 0.0)
        st = state_ref[h]
        oh = _dot(_bf(att), vh) + _dot_nt(q_dec[:, ks], _bf(st))
        state_ref[h] = st * chunk_decay[:, ks] + _dot_tn(vh, k_dec[:, ks])
        ms = jnp.mean(oh * oh, axis=-1, keepdims=True)
        outs.append(oh * lax.rsqrt(ms + GLA_NORM_EPS) * ng_ref[...])
    o = jnp.concatenate(outs, axis=1)
    y_ref[...] = (o * (go * _sigmoid(go))).astype(y_ref.dtype)


def _gla_mix(p, go, w_a2, b_a, norm_g, *, batch, seq, heads, dk, dv):
    t, width = p.shape
    e = heads * dv
    wa2 = jnp.pad(w_a2, ((0, V7X_LANES - w_a2.shape[0]), (0, 0))).astype(BF16)
    nc = seq // CHUNK
    blk = _nbytes((CHUNK, width), F32) + _nbytes((CHUNK, e), F32) + _nbytes((CHUNK, e), BF16) \
        + _nbytes(wa2.shape, wa2.dtype) + 16 * _nbytes((CHUNK, e), F32)
    return pl.pallas_call(
        functools.partial(_gla_kernel, heads=heads, dk=dk, dv=dv),
        out_shape=jax.ShapeDtypeStruct((t, e), BF16),
        grid=(batch, nc),
        in_specs=[pl.BlockSpec((CHUNK, width), lambda b, i: (b * nc + i, 0)),
                  pl.BlockSpec((CHUNK, e), lambda b, i: (b * nc + i, 0)),
                  pl.BlockSpec(wa2.shape, lambda b, i: (0, 0)),
                  pl.BlockSpec((1, heads * dk), lambda b, i: (0, 0)),
                  pl.BlockSpec((1, dv), lambda b, i: (0, 0))],
        out_specs=pl.BlockSpec((CHUNK, e), lambda b, i: (b * nc + i, 0)),
        scratch_shapes=[pltpu.VMEM((heads, dv, dk), F32)],
        compiler_params=_params(2, blk),
        name="gla_mix",
    )(p, go, wa2, b_a.reshape(1, -1), norm_g.reshape(1, -1))


def _causal_conv3(u, prev8, w):
    return _shift_rows(u, prev8, 2) * w[0:1] + _shift_rows(u, prev8, 1) * w[1:2] + u * w[2:3]


def _sconv_kernel(bg_ref, cg_ref, hh_ref, w_ref, y_ref, carry_ref):
    @pl.when(pl.program_id(1) == 0)
    def _():
        carry_ref[...] = jnp.zeros_like(carry_ref)

    u = cg_ref[...] * hh_ref[...]
    y = bg_ref[...] * _causal_conv3(u, carry_ref[...], w_ref[...])
    carry_ref[...] = u[u.shape[0] - V7X_SUBLANES:]
    y_ref[...] = y.astype(y_ref.dtype)


def _sconv_mix(p, conv_w, *, batch, seq, e):
    t = p.shape[0]
    tm = _pick_tile(seq, 512, V7X_SUBLANES)
    ms = seq // tm
    col = lambda g: pl.BlockSpec((tm, e), lambda b, i, g=g: (b * ms + i, g))
    blk = 3 * _nbytes((tm, e), F32) + _nbytes((tm, e), BF16) + 4 * _nbytes((tm, e), F32)
    return pl.pallas_call(
        _sconv_kernel,
        out_shape=jax.ShapeDtypeStruct((t, e), BF16),
        grid=(batch, ms),
        in_specs=[col(0), col(1), col(2), pl.BlockSpec((CONV_WIDTH, e), lambda b, i: (0, 0))],
        out_specs=pl.BlockSpec((tm, e), lambda b, i: (b * ms + i, 0)),
        scratch_shapes=[pltpu.VMEM((V7X_SUBLANES, e), F32)],
        compiler_params=_params(2, blk),
        name="short_conv_mix",
    )(p, p, p, conv_w)


def _merge_kernel(ya_ref, yb_ref, yc_ref, wa_ref, wb_ref, wc_ref, ga_ref, gb_ref, gc_ref, o_ref, wbf_ref):
    @pl.when(pl.program_id(1) == 0)
    def _():
        for g, w_ref in enumerate((wa_ref, wb_ref, wc_ref)):
            _cast_rows(w_ref, wbf_ref.at[g])

    acc = ga_ref[...].astype(F32) * _dot(ya_ref[...], wbf_ref[0])
    acc = acc + gb_ref[...].astype(F32) * _dot(yb_ref[...], wbf_ref[1])
    acc = acc + gc_ref[...].astype(F32) * _dot(yc_ref[...], wbf_ref[2])
    o_ref[...] = acc.astype(o_ref.dtype)


def _merge(ya, yb, yc, w_br, layer, gates):
    t, e = ya.shape
    d = w_br.shape[-1]
    tm = _pick_tile(t, 512, V7X_SUBLANES)
    tn = _pick_tile(d, 512)
    nt = d // tn
    ysp = pl.BlockSpec((tm, e), lambda j, i: (i, 0))
    wsp = lambda g: pl.BlockSpec((None, None, e, tn), lambda j, i, g=g: (layer, g, 0, j))
    gsp = lambda g: pl.BlockSpec((tm, tn), lambda j, i, g=g: (i, g * nt + j))
    blk = 3 * _nbytes((tm, e), BF16) + 3 * _nbytes((e, tn), F32) + 3 * _nbytes((tm, tn), gates.dtype) \
        + _nbytes((tm, tn), BF16) + 2 * _nbytes((tm, tn), F32) + 3 * _nbytes((e, tn), BF16) // 2
    return pl.pallas_call(
        _merge_kernel,
        out_shape=jax.ShapeDtypeStruct((t, d), BF16),
        grid=(nt, t // tm),
        in_specs=[ysp, ysp, ysp, wsp(0), wsp(1), wsp(2), gsp(0), gsp(1), gsp(2)],
        out_specs=pl.BlockSpec((tm, tn), lambda j, i: (i, j)),
        scratch_shapes=[pltpu.VMEM((3, e, tn), BF16)],
        compiler_params=_params(2, blk),
        name="gated_merge",
    )(ya, yb, yc, w_br, w_br, w_br, gates, gates, gates)


FFN_PIPE_CHUNKS = 16


def _exact_zero_like_row(x):
    r, n = x.shape
    top = jnp.max(x.reshape(r // V7X_SUBLANES, V7X_SUBLANES, n), axis=0)
    bits = pltpu.bitcast(top, jnp.uint32)
    z = lax.shift_right_logical(lax.shift_right_logical(bits, jnp.uint32(16)), jnp.uint32(16))
    return pltpu.bitcast(z, F32)[0:1].astype(BF16)


def _ffn_up_kernel(h_ref, wg_ref, wv_ref, cg_ref, cv_ref, act_ref, wbf_ref, stage_ref, carry_ref, *, ms, bm):
    j = pl.program_id(0)
    slot = lax.rem(j, 2)

    @pl.when(j == 0)
    def _():
        stage_ref[...] = jnp.zeros_like(stage_ref)
        carry_ref[...] = jnp.zeros_like(carry_ref)

    @pl.when(lax.rem(j, bm) == 0)
    def _():
        _cast_rows(wg_ref, wbf_ref.at[0])
        _cast_rows(wv_ref, wbf_ref.at[1])

    jp = jnp.maximum(j - 1, 0)
    first_rows = lax.rem(jp, ms) == 0
    prev_g = jnp.where(first_rows, 0.0, carry_ref[0])
    prev_v = jnp.where(first_rows, 0.0, carry_ref[1])

    tm, d = h_ref.shape
    rows = tm // FFN_PIPE_CHUNKS
    kc = d // FFN_PIPE_CHUNKS
    acc_g = None
    acc_v = None
    for c in range(FFN_PIPE_CHUNKS):
        rs = slice(c * rows, (c + 1) * rows)
        ug = stage_ref[1 - slot, 0, rs, :]
        uv = stage_ref[1 - slot, 1, rs, :]
        xg = _causal_conv3(ug, prev_g, cg_ref[...])
        xv = _causal_conv3(uv, prev_v, cv_ref[...])
        gelu = 0.5 * xg * (1.0 + jnp.tanh(0.7978845608028654 * (xg + 0.044715 * (xg * xg * xg))))
        act = gelu * xv
        act_ref[rs, :] = act.astype(act_ref.dtype)
        prev_g = ug[rows - V7X_SUBLANES:]
        prev_v = uv[rows - V7X_SUBLANES:]

        zero = _exact_zero_like_row(act)
        ks = slice(c * kc, (c + 1) * kc)
        hk = h_ref[:, ks]
        pg = _dot(hk, wbf_ref[0, ks, :] + zero)
        pv = _dot(hk, wbf_ref[1, ks, :] + zero)
        acc_g = pg if acc_g is None else acc_g + pg
        acc_v = pv if acc_v is None else acc_v + pv
    carry_ref[0] = prev_g
    carry_ref[1] = prev_v
    stage_ref[slot, 0] = acc_g
    stage_ref[slot, 1] = acc_v


def _ffn_up(h, w_up, layer, conv_w, *, batch, seq):
    t, d = h.shape
    f = w_up.shape[-1] // 2
    tm = _pick_tile(seq, 1024, V7X_SUBLANES)
    tn = _pick_tile(f, 256)
    ms, nt = seq // tm, f // tn
    bm = batch * ms
    last = nt * bm - 1
    cur = lambda j: jnp.minimum(j, last)
    prv = lambda j: jnp.maximum(j - 1, 0)
    assert tm % (FFN_PIPE_CHUNKS * V7X_SUBLANES) == 0 and d % (FFN_PIPE_CHUNKS * V7X_LANES) == 0
    blk = _nbytes((tm, d), BF16) + 2 * _nbytes((d, tn), F32) + _nbytes((tm, tn), BF16) \
        + 4 * _nbytes((tm, tn), F32) + _nbytes((d, tn), BF16)
    return pl.pallas_call(
        functools.partial(_ffn_up_kernel, ms=ms, bm=bm),
        out_shape=jax.ShapeDtypeStruct((t, f), BF16),
        grid=(nt * bm + 1,),
        in_specs=[pl.BlockSpec((tm, d), lambda j: (lax.rem(cur(j), bm), 0)),
                  pl.BlockSpec((None, d, tn), lambda j: (layer, 0, lax.div(cur(j), bm))),
                  pl.BlockSpec((None, d, tn), lambda j: (layer, 0, nt + lax.div(cur(j), bm))),
                  pl.BlockSpec((CONV_WIDTH, tn), lambda j: (0, lax.div(prv(j), bm))),
                  pl.BlockSpec((CONV_WIDTH, tn), lambda j: (0, nt + lax.div(prv(j), bm)))],
        out_specs=pl.BlockSpec((tm, tn), lambda j: (lax.rem(prv(j), bm), lax.div(prv(j), bm))),
        scratch_shapes=[pltpu.VMEM((2, d, tn), BF16),
                        pltpu.VMEM((2, 2, tm, tn), F32),
                        pltpu.VMEM((2, V7X_SUBLANES, tn), F32)],
        compiler_params=_params(1, blk),
        name="ffn_up_conv_gate",
    )(h, w_up, w_up, conv_w, conv_w)


def _layer(x, h, l, prm, next_pre_g, *, batch, seq):
    (w_in, rwkv_mu, rwkv_w0, rwkv_w_w2, rwkv_a0, rwkv_w_a2, rwkv_w_g2, rwkv_k_k, rwkv_k_a, rwkv_r_k,
     rwkv_ln_w, rwkv_ln_b, gla_w_a2, gla_b_a, gla_norm_g, sconv_w, w_br, w_o, post_mix_g, pre_ffn_g,
     w_up, ffn_conv_w, w_down, post_ffn_g) = prm
    d = x.shape[1]
    e = w_br.shape[2]
    lora = rwkv_w_w2.shape[1] + rwkv_w_a2.shape[1] + rwkv_w_g2.shape[1]
    gr, kw = gla_w_a2.shape[1:]
    dv = gla_norm_g.shape[1]
    heads = e // dv
    dk = kw // heads

    o_gla = 3 * e + lora
    o_go = o_gla + 2 * kw + e + gr
    o_conv = o_go + e
    o_gate = o_conv + 3 * e

    def in_proj(start, need, name, **kw_):
        width, tn = _tile_columns(need)
        return _proj(h, w_in, l, start, width, w_is_nk=True, tn=tn, name=name, **kw_)

    p_rwkv = in_proj(0, o_gla, "proj_rwkv", out_dtype=F32)
    p_gla = in_proj(o_gla, 2 * kw + e + V7X_LANES, "proj_gla", out_dtype=F32)
    p_go = in_proj(o_go, e, "proj_gla_gate", out_dtype=F32)
    p_conv = in_proj(o_conv, 3 * e, "proj_conv", out_dtype=F32)
    gates = in_proj(o_gate, 3 * d, "proj_gate", out_dtype=BF16, act="sigmoid")

    y_a = _rwkv_mix(p_rwkv, rwkv_mu[l], rwkv_w0[l], rwkv_w_w2[l], rwkv_a0[l], rwkv_w_a2[l], rwkv_w_g2[l],
                    rwkv_k_k[l], rwkv_k_a[l], rwkv_r_k[l], rwkv_ln_w[l], rwkv_ln_b[l],
                    batch=batch, seq=seq, e=e)
    y_b = _gla_mix(p_gla, p_go, gla_w_a2[l], gla_b_a[l], gla_norm_g[l],
                   batch=batch, seq=seq, heads=heads, dk=dk, dv=dv)
    y_c = _sconv_mix(p_conv, sconv_w[l], batch=batch, seq=seq, e=e)

    merged = _merge(y_a, y_b, y_c, w_br, l, gates)
    o = _proj(merged, w_o, l, 0, d, w_is_nk=False, tn=_tile_columns(d)[1], out_dtype=BF16, name="proj_out")
    x, h2 = _residual_norm(o, x, post_mix_g[l], pre_ffn_g[l])

    act = _ffn_up(h2, w_up, l, ffn_conv_w[l], batch=batch, seq=seq)
    o2 = _matmul(act, w_down, l, out_dtype=BF16, tm=512, tn=512, name="ffn_down")
    return _residual_norm(o2, x, post_ffn_g[l], next_pre_g)


def kernel(x, pre_mix_g, w_in, rwkv_mu, rwkv_w0, rwkv_w_w2, rwkv_a0, rwkv_w_a2, rwkv_w_g2, rwkv_k_k, rwkv_k_a, rwkv_r_k, rwkv_ln_w, rwkv_ln_b, gla_w_a2, gla_b_a, gla_norm_g, sconv_w, w_br, w_o, post_mix_g, pre_ffn_g, w_up, ffn_conv_w, w_down, post_ffn_g):
    batch, seq, d = x.shape
    depth = w_in.shape[0]
    prm = (jnp.swapaxes(w_in, 1, 2), rwkv_mu, rwkv_w0, rwkv_w_w2, rwkv_a0, rwkv_w_a2, rwkv_w_g2, rwkv_k_k, rwkv_k_a,
           rwkv_r_k.reshape(depth, -1), rwkv_ln_w, rwkv_ln_b, gla_w_a2, gla_b_a, gla_norm_g, sconv_w,
           w_br, w_o, post_mix_g, pre_ffn_g, w_up, ffn_conv_w, w_down.astype(BF16), post_ffn_g)
    xf = x.reshape(batch * seq, d)
    h = _rms_norm_bf16(xf, pre_mix_g[0])
    for l in range(depth):
        next_g = pre_mix_g[l + 1] if l + 1 < depth else None
        xf, h = _layer(xf, h, l, prm, next_g, batch=batch, seq=seq)
    return xf.reshape(batch, seq, d)
```

```python
import functools

import jax
import jax.numpy as jnp
from jax import lax
from jax.experimental import pallas as pl
from jax.experimental.pallas import tpu as pltpu

F32 = jnp.float32
BF16 = jnp.bfloat16

NORM_EPS = 1e-6
RWKV_HEAD_DIM = 64
RWKV_GN_EPS = 1e-5 * RWKV_HEAD_DIM
GLA_GATE_NORM = 16.0
GLA_NORM_EPS = 1e-5
CONV_WIDTH = 3
CHUNK = 64

V7X_LANES = 128
V7X_SUBLANES = 8
V7X_VMEM_BYTES = 64 * 1024 * 1024
VMEM_CAP_BYTES = V7X_VMEM_BYTES - 6 * 1024 * 1024

CAST_VREGS = 256


def _round_up(n, m):
    return (n + m - 1) // m * m


def _pick_tile(n, pref, quantum=V7X_LANES):
    if n <= pref:
        return n
    t = pref - pref % quantum
    while t >= quantum:
        if n % t == 0:
            return t
        t -= quantum
    return n


def _params(n_axes, block_bytes):
    limit = min(VMEM_CAP_BYTES, 2 * block_bytes + 16 * 1024 * 1024)
    return pltpu.CompilerParams(
        dimension_semantics=("arbitrary",) * n_axes, vmem_limit_bytes=int(limit))


def _nbytes(shape, dtype):
    n = 1
    for s in shape:
        n *= s
    return n * jnp.dtype(dtype).itemsize


def _dot(a, b):
    return jnp.dot(a, b, preferred_element_type=F32)


def _dot_nt(a, b):
    return lax.dot_general(a, b, (((1,), (1,)), ((), ())), preferred_element_type=F32)


def _dot_tn(a, b):
    return lax.dot_general(a, b, (((0,), (0,)), ((), ())), preferred_element_type=F32)


def _bf(x):
    return x.astype(BF16)


def _sigmoid(x):
    return 1.0 / (1.0 + jnp.exp(-x))


def _chunk_cumsum(x, tril_bf):
    hi = _bf(x)
    lo = _bf(x - hi.astype(F32))
    return _dot(tril_bf, hi) + _dot(tril_bf, lo)


def _shift_rows(u, prev8, k):
    r, n = u.shape
    g = r // V7X_SUBLANES
    rolled = pltpu.roll(u.reshape(g, V7X_SUBLANES, n), k, 1)
    above = jnp.concatenate([pltpu.roll(prev8, k, 0)[None], rolled[:g - 1]], axis=0)
    sub = lax.broadcasted_iota(jnp.int32, (g, V7X_SUBLANES, n), 1)
    return jnp.where(sub < k, above, rolled).reshape(r, n)


def _cast_rows(src_ref, dst_ref):
    rows, width = src_ref.shape
    step = _pick_tile(rows, max(CAST_VREGS * V7X_SUBLANES * V7X_LANES // width, 2 * V7X_SUBLANES),
                      2 * V7X_SUBLANES)
    for r in range(0, rows, step):
        dst_ref[r:r + step, :] = _bf(src_ref[r:r + step, :])


def _rms_kernel(x_ref, g_ref, h_ref):
    x = x_ref[...]
    ms = jnp.mean(x * x, axis=-1, keepdims=True)
    h_ref[...] = (x * lax.rsqrt(ms + NORM_EPS) * g_ref[...]).astype(h_ref.dtype)


def _rms_norm_bf16(x, g):
    t, d = x.shape
    tm = _pick_tile(t, 256, V7X_SUBLANES)
    blk = _nbytes((tm, d), F32) + _nbytes((tm, d), BF16)
    return pl.pallas_call(
        _rms_kernel,
        out_shape=jax.ShapeDtypeStruct((t, d), BF16),
        grid=(t // tm,),
        in_specs=[pl.BlockSpec((tm, d), lambda i: (i, 0)),
                  pl.BlockSpec((1, d), lambda i: (0, 0))],
        out_specs=pl.BlockSpec((tm, d), lambda i: (i, 0)),
        compiler_params=_params(1, blk),
        name="rms_norm",
    )(x, g.reshape(1, d))


def _residual_kernel(o_ref, x_ref, gpost_ref, *rest, with_next):
    o = o_ref[...].astype(F32)
    ms = jnp.mean(o * o, axis=-1, keepdims=True)
    xn = x_ref[...] + o * lax.rsqrt(ms + NORM_EPS) * gpost_ref[...]
    if with_next:
        gpre_ref, xo_ref, h_ref = rest
        ms2 = jnp.mean(xn * xn, axis=-1, keepdims=True)
        h_ref[...] = (xn * lax.rsqrt(ms2 + NORM_EPS) * gpre_ref[...]).astype(h_ref.dtype)
    else:
        (xo_ref,) = rest
    xo_ref[...] = xn


def _residual_norm(o, x, g_post, g_next):
    t, d = x.shape
    tm = _pick_tile(t, 256, V7X_SUBLANES)
    with_next = g_next is not None
    row = pl.BlockSpec((tm, d), lambda i: (i, 0))
    vec = pl.BlockSpec((1, d), lambda i: (0, 0))
    in_specs = [row, row, vec] + ([vec] if with_next else [])
    args = [o, x, g_post.reshape(1, d)] + ([g_next.reshape(1, d)] if with_next else [])
    out_shape = [jax.ShapeDtypeStruct((t, d), F32)]
    out_specs = [row]
    if with_next:
        out_shape.append(jax.ShapeDtypeStruct((t, d), BF16))
        out_specs.append(row)
    blk = 3 * _nbytes((tm, d), F32) + _nbytes((tm, d), BF16)
    res = pl.pallas_call(
        functools.partial(_residual_kernel, with_next=with_next),
        out_shape=out_shape,
        grid=(t // tm,),
        in_specs=in_specs,
        out_specs=out_specs,
        compiler_params=_params(1, blk),
        name="residual_norm",
    )(*args)
    return (res[0], res[1]) if with_next else (res[0], None)


def _mm_kernel(x_ref, w_ref, o_ref):
    o_ref[...] = _dot(x_ref[...], w_ref[...]).astype(o_ref.dtype)


def _matmul(x, w, layer, *, out_dtype, tm, tn, name):
    m, k = x.shape
    n = w.shape[2]
    tm = _pick_tile(m, tm, V7X_SUBLANES)
    tn = _pick_tile(n, tn)
    blk = _nbytes((tm, k), x.dtype) + _nbytes((k, tn), w.dtype) + _nbytes((tm, tn), out_dtype) \
        + _nbytes((tm, tn), F32)
    return pl.pallas_call(
        _mm_kernel,
        out_shape=jax.ShapeDtypeStruct((m, n), out_dtype),
        grid=(m // tm, n // tn),
        in_specs=[pl.BlockSpec((tm, k), lambda i, j: (i, 0)),
                  pl.BlockSpec((None, k, tn), lambda i, j: (layer, 0, j))],
        out_specs=pl.BlockSpec((tm, tn), lambda i, j: (i, j)),
        compiler_params=_params(2, blk),
        name=name,
    )(x, w)


PROJ_TN_CHOICES = (1024, 768, 512)


def _tile_columns(need):
    return min(((_round_up(need, tn), tn) for tn in PROJ_TN_CHOICES), key=lambda wt: (wt[0], -wt[1]))


def _proj_kernel(x_ref, w_ref, o_ref, wbf_ref, *, w_is_nk, act):
    @pl.when(pl.program_id(1) == 0)
    def _():
        _cast_rows(w_ref, wbf_ref)

    acc = _dot_nt(x_ref[...], wbf_ref[...]) if w_is_nk else _dot(x_ref[...], wbf_ref[...])
    if act == "sigmoid":
        acc = _sigmoid(acc)
    o_ref[...] = acc.astype(o_ref.dtype)


def _proj(x, w, layer, start, width, *, w_is_nk, tn, out_dtype, act=None, name):
    m, k = x.shape
    assert width % tn == 0
    out_bytes = _nbytes((1, tn), out_dtype)
    w_bytes = _nbytes((k, tn), F32)

    def fits(t):
        need = 2 * (_nbytes((t, k), x.dtype) + w_bytes + t * out_bytes) + w_bytes // 2 + _nbytes((t, tn), F32)
        return need <= VMEM_CAP_BYTES - 4 * 1024 * 1024

    tm = next((t for t in (1024, 512, 256, 128) if m % t == 0 and fits(t)), m)
    if w_is_nk:
        assert start % V7X_SUBLANES == 0 and start + width <= w.shape[1]
        wspec = pl.BlockSpec((pl.Squeezed(), pl.Element(tn), pl.Element(k)),
                             lambda j, i: (layer, pl.multiple_of(start + j * tn, V7X_SUBLANES), 0))
        wshape = (tn, k)
    else:
        assert start % tn == 0 and start + width <= w.shape[2]
        wspec = pl.BlockSpec((None, k, tn), lambda j, i: (layer, 0, start // tn + j))
        wshape = (k, tn)
    blk = _nbytes((tm, k), x.dtype) + w_bytes + tm * out_bytes + (_nbytes((tm, tn), F32) + w_bytes // 2) // 2
    return pl.pallas_call(
        functools.partial(_proj_kernel, w_is_nk=w_is_nk, act=act),
        out_shape=jax.ShapeDtypeStruct((m, width), out_dtype),
        grid=(width // tn, m // tm),
        in_specs=[pl.BlockSpec((tm, k), lambda j, i: (i, 0)), wspec],
        out_specs=pl.BlockSpec((tm, tn), lambda j, i: (i, j)),
        scratch_shapes=[pltpu.VMEM(wshape, BF16)],
        compiler_params=_params(2, blk),
        name=name,
    )(x, w)


def _rwkv_kernel(p_ref, mu_ref, w0_ref, ww2_ref, a0_ref, wa2_ref, wg2_ref, kk_ref, ka_ref, rk_ref,
                 lnw_ref, lnb_ref, y_ref, prev_ref, state_ref, *, e, win_w, win_a, win_g):
    c = p_ref.shape[0]
    pair = 2 * RWKV_HEAD_DIM
    n_pairs = e // pair

    @pl.when(pl.program_id(1) == 0)
    def _():
        prev_ref[...] = jnp.zeros_like(prev_ref)
        state_ref[...] = jnp.zeros_like(state_ref)

    p = p_ref[...]
    shifted = _shift_rows(p, prev_ref[...], 1)
    prev_ref[...] = p[c - V7X_SUBLANES:]
    xm = p + (shifted - p) * mu_ref[...]

    r = xm[:, 0:e]
    k = xm[:, e:2 * e]
    v = xm[:, 2 * e:3 * e]
    wd = xm[:, win_w[0]:win_w[1]]
    ad = xm[:, win_a[0]:win_a[1]]
    gd = xm[:, win_g[0]:win_g[1]]

    w_pre = w0_ref[...] + _dot(_bf(jnp.tanh(wd)), ww2_ref[...])
    lw = (-0.6065306597126334) * _sigmoid(w_pre)
    a = _sigmoid(a0_ref[...] + _dot(_bf(ad), wa2_ref[...]))
    g = _dot(_bf(_sigmoid(gd)), wg2_ref[...])

    ri = lax.broadcasted_iota(jnp.int32, (pair, pair), 0)
    ci = lax.broadcasted_iota(jnp.int32, (pair, pair), 1)
    same_head = (ri < RWKV_HEAD_DIM) == (ci < RWKV_HEAD_DIM)
    head_ones = jnp.where(same_head, 1.0, 0.0).astype(BF16)

    def head_sum(x):
        xb = _bf(x)
        return jnp.concatenate(
            [_dot(xb[:, j * pair:(j + 1) * pair], head_ones) for j in range(n_pairs)], axis=1)

    kk = k * kk_ref[...]
    kmod = k * (1.0 + (a - 1.0) * ka_ref[...])
    sums = head_sum(jnp.concatenate([kk * kk, r * kmod * rk_ref[...]], axis=0))
    kk = kk / jnp.maximum(jnp.sqrt(sums[:c]), 1e-12)
    bonus_scale = sums[c:]
    bvec = kk * a

    tri_r = lax.broadcasted_iota(jnp.int32, (c, c), 0)
    tri_c = lax.broadcasted_iota(jnp.int32, (c, c), 1)
    tril = jnp.where(tri_c <= tri_r, 1.0, 0.0).astype(BF16)
    gcum = _chunk_cumsum(lw, tril)
    glast = gcum[c - 1:c, :]
    e_pos = jnp.exp(gcum)
    e_neg = jnp.exp(-gcum)
    e_tail = jnp.exp(glast - gcum)
    r_t = r * e_pos
    kap_t = kk * jnp.exp(gcum - lw)
    b_t = bvec * e_neg
    k_t = kmod * e_neg
    b_dec = bvec * e_tail
    k_dec = kmod * e_tail
    gamma_last = jnp.exp(glast)

    wt = lax.broadcasted_iota(jnp.int32, (c, pair), 0)
    wl = lax.broadcasted_iota(jnp.int32, (c, pair), 1)
    ws = wl & (RWKV_HEAD_DIM - 1)
    lane_lo = wl < RWKV_HEAD_DIM
    m_strict = ws < wt
    m_incl = ws <= wt
    m_16 = m_strict & ((ws >> 4) == (wt >> 4))
    m_32 = m_strict & ((ws >> 5) == (wt >> 5)) & ((ws >> 4) != (wt >> 4))
    m_64 = m_strict & ((ws >> 5) != (wt >> 5))
    eye_w = jnp.where(ws == wt, 1.0, 0.0).astype(F32)
    zero_b = jnp.zeros((c, pair), BF16)

    def stack2(x):
        xb = _bf(x)
        return jnp.concatenate([jnp.where(lane_lo, xb, zero_b), jnp.where(lane_lo, zero_b, xb)], axis=0)

    pairs = range(n_pairs)
    cols = [slice(j * pair, (j + 1) * pair) for j in pairs]
    lhs = [_bf(jnp.concatenate([kap_t[:, s], r_t[:, s]], axis=0)) for s in cols]
    bk_s = [jnp.concatenate([stack2(b_t[:, s]), stack2(k_t[:, s])], axis=0) for s in cols]
    v_s = [stack2(v[:, s]) for s in cols]
    scores = [_dot_nt(lhs[j], bk_s[j]) for j in pairs]
    st = [state_ref[j] for j in pairs]
    proj = [_dot_nt(lhs[j], _bf(st[j])) for j in pairs]
    a_w = [jnp.where(m_strict, scores[j][:c, :pair], 0.0) for j in pairs]
    b_w = [_bf(jnp.where(m_strict, scores[j][:c, pair:], 0.0)) for j in pairs]
    r_w = [_bf(jnp.concatenate([jnp.where(m_incl, scores[j][c:, :pair], 0.0),
                                jnp.where(m_incl, scores[j][c:, pair:], 0.0)], axis=1)) for j in pairs]
    rhs_u = [-(proj[j][:c] + _dot(b_w[j], v_s[j])) for j in pairs]

    d1 = [jnp.where(m_16, a_w[j], 0.0) for j in pairs]
    e32 = [stack2(jnp.where(m_32, a_w[j], 0.0)) for j in pairs]
    e64 = [_bf(jnp.where(m_64, a_w[j], 0.0)) for j in pairs]
    t_inv = [eye_w - d1[j] for j in pairs]
    d2 = [_dot(_bf(d1[j]), stack2(d1[j])) for j in pairs]
    nxt = [_dot(_bf(jnp.concatenate([t_inv[j], d2[j]], axis=0)), stack2(d2[j])) for j in pairs]
    t_inv = [t_inv[j] + nxt[j][:c] for j in pairs]
    d4 = [nxt[j][c:] for j in pairs]
    nxt = [_dot(_bf(jnp.concatenate([t_inv[j], d4[j]], axis=0)), stack2(d4[j])) for j in pairs]
    t_inv = [t_inv[j] + nxt[j][:c] for j in pairs]
    t_inv = [t_inv[j] + _dot(_bf(t_inv[j]), stack2(nxt[j][c:])) for j in pairs]
    te = [_dot(_bf(t_inv[j]), e32[j]) for j in pairs]
    t32 = [_bf(t_inv[j] - _dot(_bf(te[j]), stack2(t_inv[j]))) for j in pairs]
    z = [_dot(t32[j], stack2(rhs_u[j])) for j in pairs]
    ez = [_dot(e64[j], stack2(z[j])) for j in pairs]
    u = [z[j] - _dot(t32[j], stack2(ez[j])) for j in pairs]

    uv_s = [jnp.concatenate([stack2(u[j]), v_s[j]], axis=0) for j in pairs]
    y_cols = [proj[j][c:] + _dot(r_w[j], uv_s[j]) for j in pairs]

    rj = lax.broadcasted_iota(jnp.int32, (pair, pair), 0)
    cj = lax.broadcasted_iota(jnp.int32, (pair, pair), 1)
    same_head_sq = (rj < RWKV_HEAD_DIM) == (cj < RWKV_HEAD_DIM)
    for j in pairs:
        s = cols[j]
        uv = _bf(jnp.concatenate([u[j], v[:, s]], axis=0))
        bk = _bf(jnp.concatenate([b_dec[:, s], k_dec[:, s]], axis=0))
        ds = jnp.where(same_head_sq, _dot_tn(uv, bk), 0.0)
        state_ref[j] = st[j] * gamma_last[:, s] + ds

    y = jnp.concatenate(y_cols, axis=1)
    inv_n = 1.0 / RWKV_HEAD_DIM
    mean = head_sum(y) * inv_n
    yc = y - mean
    var = head_sum(yc * yc) * inv_n
    yn = yc * lax.rsqrt(var + RWKV_GN_EPS) * lnw_ref[...] + lnb_ref[...]
    y_ref[...] = ((yn + bonus_scale * v) * g).astype(y_ref.dtype)


def _lane_window(start, length):
    return (start // V7X_LANES * V7X_LANES, _round_up(start + length, V7X_LANES))


def _rows_in_window(w, window, start):
    top = start - window[0]
    return jnp.pad(w, ((top, window[1] - window[0] - top - w.shape[0]), (0, 0))).astype(BF16)


def _rwkv_mix(p, mu, w0, w_w2, a0, w_a2, w_g2, k_k, k_a, r_k, ln_w, ln_b, *, batch, seq, e):
    t, width = p.shape
    lw, la, lg = w_w2.shape[0], w_a2.shape[0], w_g2.shape[0]
    assert CHUNK == RWKV_HEAD_DIM and e % (2 * RWKV_HEAD_DIM) == 0 and seq % CHUNK == 0
    win_w = _lane_window(3 * e, lw)
    win_a = _lane_window(3 * e + lw, la)
    win_g = _lane_window(3 * e + lw + la, lg)
    ww2 = _rows_in_window(w_w2, win_w, 3 * e)
    wa2 = _rows_in_window(w_a2, win_a, 3 * e + lw)
    wg2 = _rows_in_window(w_g2, win_g, 3 * e + lw + la)
    mu_p = jnp.pad(mu, (0, width - mu.shape[0])).reshape(1, width)
    row = lambda a: a.reshape(1, -1)
    nc = seq // CHUNK
    n_pairs = e // (2 * RWKV_HEAD_DIM)
    vec = lambda n: pl.BlockSpec((1, n), lambda b, i: (0, 0))
    full = lambda a: pl.BlockSpec(a.shape, lambda b, i: (0, 0))
    blk = _nbytes((CHUNK, width), F32) + _nbytes((CHUNK, e), BF16) \
        + sum(_nbytes(a.shape, a.dtype) for a in (ww2, wa2, wg2)) + 24 * _nbytes((CHUNK, e), F32)
    return pl.pallas_call(
        functools.partial(_rwkv_kernel, e=e, win_w=win_w, win_a=win_a, win_g=win_g),
        out_shape=jax.ShapeDtypeStruct((t, e), BF16),
        grid=(batch, nc),
        in_specs=[pl.BlockSpec((CHUNK, width), lambda b, i: (b * nc + i, 0)),
                  vec(width), vec(e), full(ww2), vec(e), full(wa2), full(wg2),
                  vec(e), vec(e), vec(e), vec(e), vec(e)],
        out_specs=pl.BlockSpec((CHUNK, e), lambda b, i: (b * nc + i, 0)),
        scratch_shapes=[pltpu.VMEM((V7X_SUBLANES, width), F32),
                        pltpu.VMEM((n_pairs, 2 * RWKV_HEAD_DIM, 2 * RWKV_HEAD_DIM), F32)],
        compiler_params=_params(2, blk),
        name="rwkv7_mix",
    )(p, mu_p, row(w0), ww2, row(a0), wa2, wg2, row(k_k), row(k_a), row(r_k), row(ln_w), row(ln_b))


def _gla_kernel(p_ref, go_ref, wa2_ref, ba_ref, ng_ref, y_ref, state_ref, *, heads, dk, dv):
    c = p_ref.shape[0]
    kw = heads * dk
    e = heads * dv

    @pl.when(pl.program_id(1) == 0)
    def _():
        state_ref[...] = jnp.zeros_like(state_ref)

    p = p_ref[...]
    q = p[:, 0:kw]
    k = p[:, kw:2 * kw]
    v = p[:, 2 * kw:2 * kw + e]
    ad = p[:, 2 * kw + e:2 * kw + e + wa2_ref.shape[0]]
    go = go_ref[...].astype(F32)

    z = _dot(_bf(ad), wa2_ref[...]) + ba_ref[...]
    gk = (jnp.minimum(z, 0.0) - jnp.log(1.0 + jnp.exp(-jnp.abs(z)))) * (1.0 / GLA_GATE_NORM)

    tri_r = lax.broadcasted_iota(jnp.int32, (c, c), 0)
    tri_c = lax.broadcasted_iota(jnp.int32, (c, c), 1)
    causal = tri_c <= tri_r
    tril = jnp.where(causal, 1.0, 0.0).astype(BF16)
    bcum = _chunk_cumsum(gk, tril)
    blast = bcum[c - 1:c, :]
    q_dec = _bf(q * (dk ** -0.5) * jnp.exp(bcum))
    k_inv = _bf(k * jnp.exp(-bcum))
    k_dec = _bf(k * jnp.exp(blast - bcum))
    chunk_decay = jnp.exp(blast)

    outs = []
    for h in range(heads):
        ks = slice(h * dk, (h + 1) * dk)
        vh = _bf(v[:, h * dv:(h + 1) * dv])
        att = jnp.where(causal, _dot_nt(q_dec[:, ks], k_inv[:, ks]), 0.0)
        st = state_ref[h]
        oh = _dot(_bf(att), vh) + _dot_nt(q_dec[:, ks], _bf(st))
        state_ref[h] = st * chunk_decay[:, ks] + _dot_tn(vh, k_dec[:, ks])
        ms = jnp.mean(oh * oh, axis=-1, keepdims=True)
        outs.append(oh * lax.rsqrt(ms + GLA_NORM_EPS) * ng_ref[...])
    o = jnp.concatenate(outs, axis=1)
    y_ref[...] = (o * (go * _sigmoid(go))).astype(y_ref.dtype)


def _gla_mix(p, go, w_a2, b_a, norm_g, *, batch, seq, heads, dk, dv):
    t, width = p.shape
    e = heads * dv
    wa2 = jnp.pad(w_a2, ((0, V7X_LANES - w_a2.shape[0]), (0, 0))).astype(BF16)
    nc = seq // CHUNK
    blk = _nbytes((CHUNK, width), F32) + _nbytes((CHUNK, e), F32) + _nbytes((CHUNK, e), BF16) \
        + _nbytes(wa2.shape, wa2.dtype) + 16 * _nbytes((CHUNK, e), F32)
    return pl.pallas_call(
        functools.partial(_gla_kernel, heads=heads, dk=dk, dv=dv),
        out_shape=jax.ShapeDtypeStruct((t, e), BF16),
        grid=(batch, nc),
        in_specs=[pl.BlockSpec((CHUNK, width), lambda b, i: (b * nc + i, 0)),
                  pl.BlockSpec((CHUNK, e), lambda b, i: (b * nc + i, 0)),
                  pl.BlockSpec(wa2.shape, lambda b, i: (0, 0)),
                  pl.BlockSpec((1, heads * dk), lambda b, i: (0, 0)),
                  pl.BlockSpec((1, dv), lambda b, i: (0, 0))],
        out_specs=pl.BlockSpec((CHUNK, e), lambda b, i: (b * nc + i, 0)),
        scratch_shapes=[pltpu.VMEM((heads, dv, dk), F32)],
        compiler_params=_params(2, blk),
        name="gla_mix",
    )(p, go, wa2, b_a.reshape(1, -1), norm_g.reshape(1, -1))


def _causal_conv3(u, prev8, w):
    return _shift_rows(u, prev8, 2) * w[0:1] + _shift_rows(u, prev8, 1) * w[1:2] + u * w[2:3]


def _sconv_kernel(bg_ref, cg_ref, hh_ref, w_ref, y_ref, carry_ref):
    @pl.when(pl.program_id(1) == 0)
    def _():
        carry_ref[...] = jnp.zeros_like(carry_ref)

    u = cg_ref[...].astype(F32) * hh_ref[...].astype(F32)
    y = bg_ref[...].astype(F32) * _causal_conv3(u, carry_ref[...], w_ref[...])
    carry_ref[...] = u[u.shape[0] - V7X_SUBLANES:]
    y_ref[...] = y.astype(y_ref.dtype)


def _sconv_mix(p, conv_w, *, batch, seq, e):
    t = p.shape[0]
    tm = _pick_tile(seq, 512, V7X_SUBLANES)
    ms = seq // tm
    col = lambda g: pl.BlockSpec((tm, e), lambda b, i, g=g: (b * ms + i, g))
    blk = 3 * _nbytes((tm, e), F32) + _nbytes((tm, e), BF16) + 4 * _nbytes((tm, e), F32)
    return pl.pallas_call(
        _sconv_kernel,
        out_shape=jax.ShapeDtypeStruct((t, e), BF16),
        grid=(batch, ms),
        in_specs=[col(0), col(1), col(2), pl.BlockSpec((CONV_WIDTH, e), lambda b, i: (0, 0))],
        out_specs=pl.BlockSpec((tm, e), lambda b, i: (b * ms + i, 0)),
        scratch_shapes=[pltpu.VMEM((V7X_SUBLANES, e), F32)],
        compiler_params=_params(2, blk),
        name="short_conv_mix",
    )(p, p, p, conv_w)


def _merge_kernel(ya_ref, yb_ref, yc_ref, wa_ref, wb_ref, wc_ref, ga_ref, gb_ref, gc_ref, o_ref, wbf_ref):
    @pl.when(pl.program_id(1) == 0)
    def _():
        for g, w_ref in enumerate((wa_ref, wb_ref, wc_ref)):
            _cast_rows(w_ref, wbf_ref.at[g])

    acc = ga_ref[...].astype(F32) * _dot(ya_ref[...], wbf_ref[0])
    acc = acc + gb_ref[...].astype(F32) * _dot(yb_ref[...], wbf_ref[1])
    acc = acc + gc_ref[...].astype(F32) * _dot(yc_ref[...], wbf_ref[2])
    o_ref[...] = acc.astype(o_ref.dtype)


def _merge(ya, yb, yc, w_br, layer, gates):
    t, e = ya.shape
    d = w_br.shape[-1]
    tm = _pick_tile(t, 512, V7X_SUBLANES)
    tn = _pick_tile(d, 512)
    nt = d // tn
    ysp = pl.BlockSpec((tm, e), lambda j, i: (i, 0))
    wsp = lambda g: pl.BlockSpec((None, None, e, tn), lambda j, i, g=g: (layer, g, 0, j))
    gsp = lambda g: pl.BlockSpec((tm, tn), lambda j, i, g=g: (i, g * nt + j))
    blk = 3 * _nbytes((tm, e), BF16) + 3 * _nbytes((e, tn), F32) + 3 * _nbytes((tm, tn), gates.dtype) \
        + _nbytes((tm, tn), BF16) + 2 * _nbytes((tm, tn), F32) + 3 * _nbytes((e, tn), BF16) // 2
    return pl.pallas_call(
        _merge_kernel,
        out_shape=jax.ShapeDtypeStruct((t, d), BF16),
        grid=(nt, t // tm),
        in_specs=[ysp, ysp, ysp, wsp(0), wsp(1), wsp(2), gsp(0), gsp(1), gsp(2)],
        out_specs=pl.BlockSpec((tm, tn), lambda j, i: (i, j)),
        scratch_shapes=[pltpu.VMEM((3, e, tn), BF16)],
        compiler_params=_params(2, blk),
        name="gated_merge",
    )(ya, yb, yc, w_br, w_br, w_br, gates, gates, gates)


FFN_PIPE_CHUNKS = 16


def _exact_zero_like_row(x):
    r, n = x.shape
    top = jnp.max(x.reshape(r // V7X_SUBLANES, V7X_SUBLANES, n), axis=0)
    bits = pltpu.bitcast(top, jnp.uint32)
    z = lax.shift_right_logical(lax.shift_right_logical(bits, jnp.uint32(16)), jnp.uint32(16))
    return pltpu.bitcast(z, F32)[0:1].astype(BF16)


def _ffn_up_kernel(h_ref, wg_ref, wv_ref, cg_ref, cv_ref, act_ref, wbf_ref, stage_ref, carry_ref, *, ms, bm):
    j = pl.program_id(0)
    slot = lax.rem(j, 2)

    @pl.when(j == 0)
    def _():
        stage_ref[...] = jnp.zeros_like(stage_ref)
        carry_ref[...] = jnp.zeros_like(carry_ref)

    @pl.when(lax.rem(j, bm) == 0)
    def _():
        _cast_rows(wg_ref, wbf_ref.at[0])
        _cast_rows(wv_ref, wbf_ref.at[1])

    jp = jnp.maximum(j - 1, 0)
    first_rows = lax.rem(jp, ms) == 0
    prev_g = jnp.where(first_rows, 0.0, carry_ref[0])
    prev_v = jnp.where(first_rows, 0.0, carry_ref[1])

    tm, d = h_ref.shape
    rows = tm // FFN_PIPE_CHUNKS
    kc = d // FFN_PIPE_CHUNKS
    acc_g = None
    acc_v = None
    for c in range(FFN_PIPE_CHUNKS):
        rs = slice(c * rows, (c + 1) * rows)
        ug = stage_ref[1 - slot, 0, rs, :]
        uv = stage_ref[1 - slot, 1, rs, :]
        xg = _causal_conv3(ug, prev_g, cg_ref[...])
        xv = _causal_conv3(uv, prev_v, cv_ref[...])
        gelu = 0.5 * xg * (1.0 + jnp.tanh(0.7978845608028654 * (xg + 0.044715 * (xg * xg * xg))))
        act = gelu * xv
        act_ref[rs, :] = act.astype(act_ref.dtype)
        prev_g = ug[rows - V7X_SUBLANES:]
        prev_v = uv[rows - V7X_SUBLANES:]

        zero = _exact_zero_like_row(act)
        ks = slice(c * kc, (c + 1) * kc)
        hk = h_ref[:, ks]
        pg = _dot(hk, wbf_ref[0, ks, :] + zero)
        pv = _dot(hk, wbf_ref[1, ks, :] + zero)
        acc_g = pg if acc_g is None else acc_g + pg
        acc_v = pv if acc_v is None else acc_v + pv
    carry_ref[0] = prev_g
    carry_ref[1] = prev_v
    stage_ref[slot, 0] = acc_g
    stage_ref[slot, 1] = acc_v


def _ffn_up(h, w_up, layer, conv_w, *, batch, seq):
    t, d = h.shape
    f = w_up.shape[-1] // 2
    tm = _pick_tile(seq, 1024, V7X_SUBLANES)
    tn = _pick_tile(f, 256)
    ms, nt = seq // tm, f // tn
    bm = batch * ms
    last = nt * bm - 1
    cur = lambda j: jnp.minimum(j, last)
    prv = lambda j: jnp.maximum(j - 1, 0)
    assert tm % (FFN_PIPE_CHUNKS * V7X_SUBLANES) == 0 and d % (FFN_PIPE_CHUNKS * V7X_LANES) == 0
    blk = _nbytes((tm, d), BF16) + 2 * _nbytes((d, tn), F32) + _nbytes((tm, tn), BF16) \
        + 4 * _nbytes((tm, tn), F32) + _nbytes((d, tn), BF16)
    return pl.pallas_call(
        functools.partial(_ffn_up_kernel, ms=ms, bm=bm),
        out_shape=jax.ShapeDtypeStruct((t, f), BF16),
        grid=(nt * bm + 1,),
        in_specs=[pl.BlockSpec((tm, d), lambda j: (lax.rem(cur(j), bm), 0)),
                  pl.BlockSpec((None, d, tn), lambda j: (layer, 0, lax.div(cur(j), bm))),
                  pl.BlockSpec((None, d, tn), lambda j: (layer, 0, nt + lax.div(cur(j), bm))),
                  pl.BlockSpec((CONV_WIDTH, tn), lambda j: (0, lax.div(prv(j), bm))),
                  pl.BlockSpec((CONV_WIDTH, tn), lambda j: (0, nt + lax.div(prv(j), bm)))],
        out_specs=pl.BlockSpec((tm, tn), lambda j: (lax.rem(prv(j), bm), lax.div(prv(j), bm))),
        scratch_shapes=[pltpu.VMEM((2, d, tn), BF16),
                        pltpu.VMEM((2, 2, tm, tn), F32),
                        pltpu.VMEM((2, V7X_SUBLANES, tn), F32)],
        compiler_params=_params(1, blk),
        name="ffn_up_conv_gate",
    )(h, w_up, w_up, conv_w, conv_w)


def _layer(x, h, l, prm, next_pre_g, *, batch, seq):
    (w_in, rwkv_mu, rwkv_w0, rwkv_w_w2, rwkv_a0, rwkv_w_a2, rwkv_w_g2, rwkv_k_k, rwkv_k_a, rwkv_r_k,
     rwkv_ln_w, rwkv_ln_b, gla_w_a2, gla_b_a, gla_norm_g, sconv_w, w_br, w_o, post_mix_g, pre_ffn_g,
     w_up, ffn_conv_w, w_down, post_ffn_g) = prm
    d = x.shape[1]
    e = w_br.shape[2]
    lora = rwkv_w_w2.shape[1] + rwkv_w_a2.shape[1] + rwkv_w_g2.shape[1]
    gr, kw = gla_w_a2.shape[1:]
    dv = gla_norm_g.shape[1]
    heads = e // dv
    dk = kw // heads

    o_gla = 3 * e + lora
    o_go = o_gla + 2 * kw + e + gr
    o_conv = o_go + e
    o_gate = o_conv + 3 * e

    def in_proj(start, need, name, **kw_):
        width, tn = _tile_columns(need)
        return _proj(h, w_in, l, start, width, w_is_nk=True, tn=tn, name=name, **kw_)

    p_rwkv = in_proj(0, o_gla, "proj_rwkv", out_dtype=F32)
    p_gla = in_proj(o_gla, 2 * kw + e + V7X_LANES, "proj_gla", out_dtype=F32)
    p_go = in_proj(o_go, e, "proj_gla_gate", out_dtype=BF16)
    p_conv = in_proj(o_conv, 3 * e, "proj_conv", out_dtype=BF16)
    gates = in_proj(o_gate, 3 * d, "proj_gate", out_dtype=BF16, act="sigmoid")

    y_a = _rwkv_mix(p_rwkv, rwkv_mu[l], rwkv_w0[l], rwkv_w_w2[l], rwkv_a0[l], rwkv_w_a2[l], rwkv_w_g2[l],
                    rwkv_k_k[l], rwkv_k_a[l], rwkv_r_k[l], rwkv_ln_w[l], rwkv_ln_b[l],
                    batch=batch, seq=seq, e=e)
    y_b = _gla_mix(p_gla, p_go, gla_w_a2[l], gla_b_a[l], gla_norm_g[l],
                   batch=batch, seq=seq, heads=heads, dk=dk, dv=dv)
    y_c = _sconv_mix(p_conv, sconv_w[l], batch=batch, seq=seq, e=e)

    merged = _merge(y_a, y_b, y_c, w_br, l, gates)
    o = _proj(merged, w_o, l, 0, d, w_is_nk=False, tn=_tile_columns(d)[1], out_dtype=BF16, name="proj_out")
    x, h2 = _residual_norm(o, x, post_mix_g[l], pre_ffn_g[l])

    act = _ffn_up(h2, w_up, l, ffn_conv_w[l], batch=batch, seq=seq)
    o2 = _matmul(act, w_down, l, out_dtype=BF16, tm=512, tn=512, name="ffn_down")
    return _residual_norm(o2, x, post_ffn_g[l], next_pre_g)


def kernel(x, pre_mix_g, w_in, rwkv_mu, rwkv_w0, rwkv_w_w2, rwkv_a0, rwkv_w_a2, rwkv_w_g2, rwkv_k_k, rwkv_k_a, rwkv_r_k, rwkv_ln_w, rwkv_ln_b, gla_w_a2, gla_b_a, gla_norm_g, sconv_w, w_br, w_o, post_mix_g, pre_ffn_g, w_up, ffn_conv_w, w_down, post_ffn_g):
    batch, seq, d = x.shape
    depth = w_in.shape[0]
    prm = (jnp.swapaxes(w_in, 1, 2), rwkv_mu, rwkv_w0, rwkv_w_w2, rwkv_a0, rwkv_w_a2, rwkv_w_g2, rwkv_k_k, rwkv_k_a,
           rwkv_r_k.reshape(depth, -1), rwkv_ln_w, rwkv_ln_b, gla_w_a2, gla_b_a, gla_norm_g, sconv_w,
           w_br, w_o, post_mix_g, pre_ffn_g, w_up, ffn_conv_w, w_down.astype(BF16), post_ffn_g)
    xf = x.reshape(batch * seq, d)
    h = _rms_norm_bf16(xf, pre_mix_g[0])
    for l in range(depth):
        next_g = pre_mix_g[l + 1] if l + 1 < depth else None
        xf, h = _layer(xf, h, l, prm, next_g, batch=batch, seq=seq)
    return xf.reshape(batch, seq, d)
```
